```python
import jax, jax.numpy as jnp
from jax import lax
import numpy as np

D_MODEL = 4096
BATCH = 2
SEQ = 4096
DEPTH = 2

N_A_LAYERS = DEPTH // 2
N_B_LAYERS = DEPTH - N_A_LAYERS
HEAD_DIM = 128
N_HEADS = D_MODEL // HEAD_DIM
CONV_WIDTH = 3
Q_BLOCK = 128
N_EXPERTS = 64
N_GROUPS = 8
TOPK_GROUPS = 4
TOP_K = 8
D_EXPERT = 384
D_SHARED = 384
ROUTED_SCALE = 2.5
ALPHA = (2.0 * DEPTH) ** 0.25
BETA = (8.0 * DEPTH) ** -0.25
LN_EPS = 1e-5

kernel_name = "yoco_shortconv_fox_moe_deepnorm"


def layer_norm(x, g, b):
    xf = x.astype(jnp.float32)
    mu = jnp.mean(xf, axis=-1, keepdims=True)
    xc = xf - mu
    var = jnp.mean(xc * xc, axis=-1, keepdims=True)
    y = xc * lax.rsqrt(var + LN_EPS) * g.astype(jnp.float32) + b.astype(jnp.float32)
    return y.astype(x.dtype)


def causal_depthwise_conv(u, w):
    return lax.conv_general_dilated(
        u, w[:, None, :].astype(u.dtype), window_strides=(1,),
        padding=[(CONV_WIDTH - 1, 0)], dimension_numbers=('NWC', 'WIO', 'NWC'),
        feature_group_count=u.shape[-1])


def conv_mixer(x, w_in, w_conv, w_out):
    out_gate, in_gate, u = jnp.split(x @ w_in, 3, axis=-1)
    y = out_gate * causal_depthwise_conv(in_gate * u, w_conv)
    return y @ w_out


def shared_kv(x, kv_w, kv_fb):
    bsz, s, d = x.shape
    proj = x @ kv_w
    k = proj[..., :d].reshape(bsz, s, N_HEADS, HEAD_DIM)
    v = proj[..., d:2 * d].reshape(bsz, s, N_HEADS, HEAD_DIM)
    f_logit = proj[..., 2 * d:].astype(jnp.float32) + kv_fb.astype(jnp.float32)
    c = jnp.cumsum(jax.nn.log_sigmoid(f_logit), axis=1)
    return k, v, jnp.transpose(c, (0, 2, 1))


def fox_attention(x, w_q, w_o, k, v, c):
    bsz, s, d = x.shape
    q = (x @ w_q).reshape(bsz, s, N_HEADS, HEAD_DIM)
    scale = HEAD_DIM ** -0.5
    outs = []
    for i in range(s // Q_BLOCK):
        q0 = i * Q_BLOCK
        end = q0 + Q_BLOCK
        logits = jnp.einsum('bqhd,bkhd->bhqk', q[:, q0:end], k[:, :end]).astype(jnp.float32) * scale
        logits = logits + c[:, :, q0:end, None] - c[:, :, None, :end]
        causal = jnp.arange(end)[None, :] <= (q0 + jnp.arange(Q_BLOCK))[:, None]
        logits = jnp.where(causal, logits, -jnp.inf)
        p = jax.nn.softmax(logits, axis=-1).astype(v.dtype)
        outs.append(jnp.einsum('bhqk,bkhd->bqhd', p, v[:, :end]))
    o = jnp.concatenate(outs, axis=1).reshape(bsz, s, d)
    return o @ w_o


def moe_ffn(x, w_r, b_r, wg, wu, wd, sg, su, sd):
    bsz, s, d = x.shape
    t = x.reshape(-1, d)
    n_tok = t.shape[0]
    scores = jax.nn.sigmoid((t @ w_r).astype(jnp.float32))
    choice = scores + b_r.astype(jnp.float32)
    grp = choice.reshape(n_tok, N_GROUPS, N_EXPERTS // N_GROUPS)
    grp_score = jnp.sum(lax.top_k(grp, 2)[0], axis=-1)
    _, grp_idx = lax.top_k(grp_score, TOPK_GROUPS)
    grp_mask = jnp.sum(jax.nn.one_hot(grp_idx, N_GROUPS, dtype=jnp.float32), axis=1) > 0
    exp_mask = jnp.repeat(grp_mask, N_EXPERTS // N_GROUPS, axis=1)
    masked = jnp.where(exp_mask, choice, -jnp.inf)
    _, idx = lax.top_k(masked, TOP_K)
    w = jnp.take_along_axis(scores, idx, axis=1)
    w = w / jnp.sum(w, axis=-1, keepdims=True) * ROUTED_SCALE
    gates = jnp.einsum('tk,tke->te', w, jax.nn.one_hot(idx, N_EXPERTS, dtype=jnp.float32)).astype(t.dtype)

    def expert_step(acc, p):
        g_e, wg_e, wu_e, wd_e = p
        h = jax.nn.silu(t @ wg_e) * (t @ wu_e)
        return acc + g_e[:, None] * (h @ wd_e), None

    routed, _ = lax.scan(expert_step, jnp.zeros_like(t), (gates.T, wg, wu, wd))
    shared = (jax.nn.silu(t @ sg) * (t @ su)) @ sd
    return (routed + shared).reshape(bsz, s, d)


def setup_inputs(seed: int = 0) -> dict:
    key = jax.random.key(seed)
    ks = jax.random.split(key, 20)
    D, H = D_MODEL, N_HEADS
    inv = D ** -0.5

    def nrm(k, shape, scale):
        return jax.random.normal(k, shape, jnp.float32) * scale

    kv_w = jnp.concatenate([
        nrm(ks[6], (D, D), inv),
        nrm(ks[7], (D, D), inv * BETA),
        nrm(ks[8], (D, H), inv),
    ], axis=1)
    return {
        "x": nrm(ks[0], (BATCH, SEQ, D), 1.0),
        "ln_g": 1.0 + nrm(ks[1], (DEPTH, 2, D), 0.02),
        "ln_b": nrm(ks[2], (DEPTH, 2, D), 0.02),
        "conv_w_in": nrm(ks[3], (N_A_LAYERS, D, 3 * D), inv),
        "conv_w": nrm(ks[4], (N_A_LAYERS, CONV_WIDTH, D), CONV_WIDTH ** -0.5),
        "conv_w_out": nrm(ks[5], (N_A_LAYERS, D, D), inv * BETA),
        "kv_w": kv_w,
        "kv_fb": 3.0 + nrm(ks[9], (H,), 0.5),
        "attn_w_q": nrm(ks[10], (N_B_LAYERS, D, D), inv),
        "attn_w_o": nrm(ks[11], (N_B_LAYERS, D, D), inv * BETA),
        "router_w": nrm(ks[12], (DEPTH, D, N_EXPERTS), inv),
        "router_b": nrm(ks[13], (DEPTH, N_EXPERTS), 0.01),
        "exp_w_gate": nrm(ks[14], (DEPTH, N_EXPERTS, D, D_EXPERT), inv),
        "exp_w_up": nrm(ks[15], (DEPTH, N_EXPERTS, D, D_EXPERT), inv),
        "exp_w_down": nrm(ks[16], (DEPTH, N_EXPERTS, D_EXPERT, D), D_EXPERT ** -0.5 * BETA),
        "shared_w_gate": nrm(ks[17], (DEPTH, D, D_SHARED), inv),
        "shared_w_up": nrm(ks[18], (DEPTH, D, D_SHARED), inv),
        "shared_w_down": nrm(ks[19], (DEPTH, D_SHARED, D), D_SHARED ** -0.5 * BETA),
    }


def reference(x, ln_g, ln_b, conv_w_in, conv_w, conv_w_out, kv_w, kv_fb, attn_w_q, attn_w_o,
              router_w, router_b, exp_w_gate, exp_w_up, exp_w_down,
              shared_w_gate, shared_w_up, shared_w_down):
    k = v = c = None
    for l in range(DEPTH):
        if l < N_A_LAYERS:
            h = conv_mixer(x, conv_w_in[l], conv_w[l], conv_w_out[l])
        else:
            if l == N_A_LAYERS:
                k, v, c = shared_kv(x, kv_w, kv_fb)
            j = l - N_A_LAYERS
            h = fox_attention(x, attn_w_q[j], attn_w_o[j], k, v, c)
        x = layer_norm(ALPHA * x + h, ln_g[l, 0], ln_b[l, 0])
        m = moe_ffn(x, router_w[l], router_b[l], exp_w_gate[l], exp_w_up[l], exp_w_down[l],
                    shared_w_gate[l], shared_w_up[l], shared_w_down[l])
        x = layer_norm(ALPHA * x + m, ln_g[l, 1], ln_b[l, 1])
    return x
```

```python
import functools

import jax
import jax.numpy as jnp
from jax import lax
from jax.experimental import pallas as pl
from jax.experimental.pallas import tpu as pltpu

D_MODEL = 4096
DEPTH = 2
N_A_LAYERS = DEPTH // 2
HEAD_DIM = 128
N_HEADS = D_MODEL // HEAD_DIM
CONV_WIDTH = 3
N_EXPERTS = 64
N_GROUPS = 8
GROUP_SIZE = N_EXPERTS // N_GROUPS
TOPK_GROUPS = 4
TOP_K = 8
D_EXPERT = 384
ROUTED_SCALE = 2.5
ALPHA = (2.0 * DEPTH) ** 0.25
LN_EPS = 1e-5

_MIB = 1024 * 1024
_VMEM_LIMIT = 56 * _MIB
_NEG = -1e30

_F32 = jnp.float32
_BF16 = jnp.bfloat16


def _params(n_axes, vmem=_VMEM_LIMIT):
    return pltpu.CompilerParams(dimension_semantics=("arbitrary",) * n_axes,
                                vmem_limit_bytes=vmem)


def _layer_norm_rows(v, g, b):
    mu = jnp.mean(v, axis=-1, keepdims=True)
    xc = v - mu
    var = jnp.mean(xc * xc, axis=-1, keepdims=True)
    return xc * lax.rsqrt(var + LN_EPS) * g + b


def _mm_kernel(a_ref, w_ref, o_ref):
    o_ref[...] = jnp.dot(a_ref[...], w_ref[...],
                         preferred_element_type=_F32).astype(o_ref.dtype)


def _mm(a, w, out_dtype, tm=512, tn=1024):
    m, k = a.shape
    n = w.shape[1]
    return pl.pallas_call(
        _mm_kernel,
        grid=(n // tn, m // tm),
        in_specs=[pl.BlockSpec((tm, k), lambda j, i: (i, 0)),
                  pl.BlockSpec((k, tn), lambda j, i: (0, j))],
        out_specs=pl.BlockSpec((tm, tn), lambda j, i: (i, j)),
        out_shape=jax.ShapeDtypeStruct((m, n), out_dtype),
        compiler_params=_params(2),
        name="dense_mm",
    )(a, w)


def _conv_in_kernel(a_ref, wb_ref, wc_ref, wu_ref, cw_ref, y_ref, carry_ref, *, tiles_per_seq):
    i = pl.program_id(1)

    @pl.when(i % tiles_per_seq == 0)
    def _():
        carry_ref[...] = jnp.zeros_like(carry_ref)

    a = a_ref[...]
    gate_out = jnp.dot(a, wb_ref[...], preferred_element_type=_F32)
    gate_in = jnp.dot(a, wc_ref[...], preferred_element_type=_F32)
    u = jnp.dot(a, wu_ref[...], preferred_element_type=_F32)
    z = gate_in * u
    tm = z.shape[0]
    row = lax.broadcasted_iota(jnp.int32, z.shape, 0)
    p1 = carry_ref[7:8, :]
    p2 = carry_ref[6:7, :]
    z1 = jnp.where(row == 0, p1, pltpu.roll(z, 1, 0))
    z2 = jnp.where(row == 0, p2, jnp.where(row == 1, p1, pltpu.roll(z, 2, 0)))
    conv = cw_ref[0:1, :] * z2 + cw_ref[1:2, :] * z1 + cw_ref[2:3, :] * z
    y_ref[...] = (gate_out * conv).astype(y_ref.dtype)
    carry_ref[...] = z[tm - 8:tm, :]


def _conv_in(a, w_in, conv_w, seq, tm=512, tn=512):
    m, k = a.shape
    d = w_in.shape[1] // 3
    nb = d // tn
    kern = functools.partial(_conv_in_kernel, tiles_per_seq=seq // tm)
    return pl.pallas_call(
        kern,
        grid=(nb, m // tm),
        in_specs=[pl.BlockSpec((tm, k), lambda j, i: (i, 0)),
                  pl.BlockSpec((k, tn), lambda j, i: (0, j)),
                  pl.BlockSpec((k, tn), lambda j, i: (0, j + nb)),
                  pl.BlockSpec((k, tn), lambda j, i: (0, j + 2 * nb)),
                  pl.BlockSpec((CONV_WIDTH, tn), lambda j, i: (0, j))],
        out_specs=pl.BlockSpec((tm, tn), lambda j, i: (i, j)),
        out_shape=jax.ShapeDtypeStruct((m, d), _BF16),
        scratch_shapes=[pltpu.VMEM((8, tn), _F32)],
        compiler_params=_params(2),
        name="conv_in",
    )(a, w_in, w_in, w_in, conv_w)


def _res_ln_kernel(x_ref, h_ref, g_ref, b_ref, o_ref, ob_ref):
    v = ALPHA * x_ref[...] + h_ref[...]
    y = _layer_norm_rows(v, g_ref[...], b_ref[...])
    o_ref[...] = y
    ob_ref[...] = y.astype(_BF16)


def _res_ln(x, h, g, b, tm=256):
    m, d = x.shape
    row = pl.BlockSpec((tm, d), lambda i: (i, 0))
    vec = pl.BlockSpec((1, d), lambda i: (0, 0))
    return pl.pallas_call(
        _res_ln_kernel,
        grid=(m // tm,),
        in_specs=[row, row, vec, vec],
        out_specs=[row, row],
        out_shape=[jax.ShapeDtypeStruct((m, d), _F32), jax.ShapeDtypeStruct((m, d), _BF16)],
        compiler_params=_params(1),
        name="res_ln",
    )(x, h, g.reshape(1, d), b.reshape(1, d))


def _first_index_of_max(vals, axis, size):
    mx = jnp.max(vals, axis=axis, keepdims=True)
    idx = lax.broadcasted_iota(jnp.int32, vals.shape, axis)
    first = jnp.min(jnp.where(vals == mx, idx, size), axis=axis, keepdims=True)
    return mx, idx == first


def _router_kernel(x_ref, wt_ref, b_ref, idx_ref, w_ref, rank_ref, cnt_ref, carry_ref):
    i = pl.program_id(0)

    @pl.when(i == 0)
    def _():
        carry_ref[...] = jnp.zeros_like(carry_ref)

    tm = x_ref.shape[0]
    logits = lax.dot_general(wt_ref[...], x_ref[...], (((1,), (1,)), ((), ())),
                             precision=lax.Precision.HIGHEST,
                             preferred_element_type=_F32)
    scores = jax.nn.sigmoid(logits)
    choice = scores + b_ref[...]
    c3 = choice.reshape(N_GROUPS, GROUP_SIZE, tm)
    m1, hit1 = _first_index_of_max(c3, 1, GROUP_SIZE)
    m2 = jnp.max(jnp.where(hit1, _NEG, c3), axis=1, keepdims=True)
    cur = (m1 + m2).reshape(N_GROUPS, tm)
    sel = jnp.zeros((N_GROUPS, tm), _F32)
    for _ in range(TOPK_GROUPS):
        _, hit = _first_index_of_max(cur, 0, N_GROUPS)
        sel = jnp.where(hit, 1.0, sel)
        cur = jnp.where(hit, _NEG, cur)
    exp_mask = jnp.broadcast_to(sel.reshape(N_GROUPS, 1, tm),
                                (N_GROUPS, GROUP_SIZE, tm)).reshape(N_EXPERTS, tm)
    masked = jnp.where(exp_mask > 0.0, choice, _NEG)

    eidx = lax.broadcasted_iota(jnp.int32, (N_EXPERTS, tm), 0)
    hits, idxs, ws = [], [], []
    member = jnp.zeros((N_EXPERTS, tm), _F32)
    for _ in range(TOP_K):
        _, hit = _first_index_of_max(masked, 0, N_EXPERTS)
        hits.append(hit)
        idxs.append(jnp.sum(jnp.where(hit, eidx, 0), axis=0, keepdims=True))
        ws.append(jnp.sum(jnp.where(hit, scores, 0.0), axis=0, keepdims=True))
        member = jnp.where(hit, 1.0, member)
        masked = jnp.where(hit, _NEG, masked)
    w = jnp.concatenate(ws, axis=0)
    w = w / jnp.sum(w, axis=0, keepdims=True) * ROUTED_SCALE
    idx_ref[...] = jnp.concatenate(idxs, axis=0)
    w_ref[...] = w

    r_i = lax.broadcasted_iota(jnp.int32, (tm, tm), 0)
    c_i = lax.broadcasted_iota(jnp.int32, (tm, tm), 1)
    before = jnp.where(r_i < c_i, 1.0, 0.0).astype(_BF16)
    prefix = jnp.dot(member.astype(_BF16), before, preferred_element_type=_F32)
    base = carry_ref[...] + prefix
    ranks = [jnp.sum(jnp.where(h, base, 0.0), axis=0, keepdims=True) for h in hits]
    rank_ref[...] = jnp.concatenate(ranks, axis=0).astype(jnp.int32)
    total = carry_ref[...] + jnp.sum(member, axis=1, keepdims=True)
    carry_ref[...] = total
    cnt_ref[...] = total.astype(jnp.int32)


def _router(x, w_r, b_r, tm=512):
    t, d = x.shape
    kt = pl.BlockSpec((TOP_K, tm), lambda i: (0, i))
    return pl.pallas_call(
        _router_kernel,
        grid=(t // tm,),
        in_specs=[pl.BlockSpec((tm, d), lambda i: (i, 0)),
                  pl.BlockSpec((N_EXPERTS, d), lambda i: (0, 0)),
                  pl.BlockSpec((N_EXPERTS, 1), lambda i: (0, 0))],
        out_specs=[kt, kt, kt, pl.BlockSpec((N_EXPERTS, 1), lambda i: (0, 0))],
        out_shape=[jax.ShapeDtypeStruct((TOP_K, t), jnp.int32),
                   jax.ShapeDtypeStruct((TOP_K, t), _F32),
                   jax.ShapeDtypeStruct((TOP_K, t), jnp.int32),
                   jax.ShapeDtypeStruct((N_EXPERTS, 1), jnp.int32)],
        scratch_shapes=[pltpu.VMEM((N_EXPERTS, 1), _F32)],
        compiler_params=_params(1),
        name="router",
    )(x, w_r.T, b_r.astype(_F32).reshape(N_EXPERTS, 1))


def _experts_kernel(te_ref, nv_ref, tok_ref, x_hbm, wg_ref, wu_ref, wd_ref, y_ref,
                    xbuf, sem, *, tm):
    i = pl.program_id(0)
    n_valid = nv_ref[0]

    def row_copy(tile, r, slot):
        tok = tok_ref[tile * tm + r]
        return pltpu.make_async_copy(x_hbm.at[pl.ds(tok, 1), :],
                                     xbuf.at[slot, pl.ds(r, 1), :], sem.at[slot])

    def start_gather(tile, slot):
        def body(r, c):
            row_copy(tile, r, slot).start()
            return c
        lax.fori_loop(0, tm, body, 0)

    @pl.when(i == 0)
    def _():
        start_gather(0, 0)

    @pl.when(i + 1 < n_valid)
    def _():
        start_gather(i + 1, (i + 1) % 2)

    @pl.when(i < n_valid)
    def _():
        slot = i % 2
        pltpu.make_async_copy(x_hbm.at[pl.ds(0, tm), :], xbuf.at[slot], sem.at[slot]).wait()
        a = xbuf[slot].astype(_BF16)
        g = jnp.dot(a, wg_ref[...], preferred_element_type=_F32)
        u = jnp.dot(a, wu_ref[...], preferred_element_type=_F32)
        h = (g * jax.nn.sigmoid(g) * u).astype(_BF16)
        y_ref[...] = jnp.dot(h, wd_ref[...], preferred_element_type=_F32).astype(y_ref.dtype)

    @pl.when(i >= n_valid)
    def _():
        y_ref[...] = jnp.zeros_like(y_ref)


def _experts(x, tile_expert, n_valid, row_token, wg, wu, wd, tm):
    t, d = x.shape
    n_tiles = tile_expert.shape[0]

    def w_map(i, te, nv, tok):
        return (te[i], 0, 0)

    def y_map(i, te, nv, tok):
        return (i, 0)

    grid_spec = pltpu.PrefetchScalarGridSpec(
        num_scalar_prefetch=3,
        grid=(n_tiles,),
        in_specs=[pl.BlockSpec(memory_space=pl.ANY),
                  pl.BlockSpec((None, d, D_EXPERT), w_map),
                  pl.BlockSpec((None, d, D_EXPERT), w_map),
                  pl.BlockSpec((None, D_EXPERT, d), w_map)],
        out_specs=pl.BlockSpec((tm, d), y_map),
        scratch_shapes=[pltpu.VMEM((2, tm, d), _F32), pltpu.SemaphoreType.DMA((2,))],
    )
    return pl.pallas_call(
        functools.partial(_experts_kernel, tm=tm),
        grid_spec=grid_spec,
        out_shape=jax.ShapeDtypeStruct((n_tiles * tm, d), _F32),
        compiler_params=_params(1),
        name="experts",
    )(tile_expert, n_valid, row_token, x, wg, wu, wd)


def _ffn_kernel(a_ref, wg_ref, wu_ref, wd_ref, o_ref):
    a = a_ref[...]
    g = jnp.dot(a, wg_ref[...], preferred_element_type=_F32)
    u = jnp.dot(a, wu_ref[...], preferred_element_type=_F32)
    h = (g * jax.nn.sigmoid(g) * u).astype(_BF16)
    o_ref[...] = jnp.dot(h, wd_ref[...], preferred_element_type=_F32)


def _shared_ffn(a, wg, wu, wd, tm=512):
    m, d = a.shape
    f = wg.shape[1]
    return pl.pallas_call(
        _ffn_kernel,
        grid=(m // tm,),
        in_specs=[pl.BlockSpec((tm, d), lambda i: (i, 0)),
                  pl.BlockSpec((d, f), lambda i: (0, 0)),
                  pl.BlockSpec((d, f), lambda i: (0, 0)),
                  pl.BlockSpec((f, d), lambda i: (0, 0))],
        out_specs=pl.BlockSpec((tm, d), lambda i: (i, 0)),
        out_shape=jax.ShapeDtypeStruct((m, d), _F32),
        compiler_params=_params(1),
        name="shared_ffn",
    )(a, wg, wu, wd)


def _combine_kernel(pos_ref, y_hbm, w_ref, sh_ref, x_ref, g_ref, b_ref, o_ref, ob_ref,
                    ybuf, sem, *, tc, n_tok):
    i = pl.program_id(0)
    n_steps = pl.num_programs(0)

    def start_gather(step, slot):
        def body(r, c):
            for k in range(TOP_K):
                p = pos_ref[k * n_tok + step * tc + r]
                pltpu.make_async_copy(y_hbm.at[pl.ds(p, 1), :],
                                      ybuf.at[slot, k, pl.ds(r, 1), :], sem.at[slot]).start()
            return c
        lax.fori_loop(0, tc, body, 0)

    @pl.when(i == 0)
    def _():
        start_gather(0, 0)

    @pl.when(i + 1 < n_steps)
    def _():
        start_gather(i + 1, (i + 1) % 2)

    slot = i % 2
    for k in range(TOP_K):
        pltpu.make_async_copy(y_hbm.at[pl.ds(0, tc), :], ybuf.at[slot, k], sem.at[slot]).wait()
    w = w_ref[...]
    acc = sh_ref[...]
    for k in range(TOP_K):
        acc = acc + w[:, k:k + 1] * ybuf[slot, k]
    v = ALPHA * x_ref[...] + acc
    y = _layer_norm_rows(v, g_ref[...], b_ref[...])
    o_ref[...] = y
    ob_ref[...] = y.astype(_BF16)


def _combine(pos_flat, ys, w_tok, shared, x, g, b, tc=64):
    t, d = x.shape
    row = pl.BlockSpec((tc, d), lambda i, pos: (i, 0))
    vec = pl.BlockSpec((1, d), lambda i, pos: (0, 0))
    grid_spec = pltpu.PrefetchScalarGridSpec(
        num_scalar_prefetch=1,
        grid=(t // tc,),
        in_specs=[pl.BlockSpec(memory_space=pl.ANY),
                  pl.BlockSpec((tc, TOP_K), lambda i, pos: (i, 0)),
                  row, row, vec, vec],
        out_specs=[row, row],
        scratch_shapes=[pltpu.VMEM((2, TOP_K, tc, d), _F32), pltpu.SemaphoreType.DMA((2,))],
    )
    return pl.pallas_call(
        functools.partial(_combine_kernel, tc=tc, n_tok=t),
        grid_spec=grid_spec,
        out_shape=[jax.ShapeDtypeStruct((t, d), _F32), jax.ShapeDtypeStruct((t, d), _BF16)],
        compiler_params=_params(1),
        name="combine",
    )(pos_flat, ys, w_tok, shared, x, g.reshape(1, d), b.reshape(1, d))


def _moe_block(x, xb, w_r, b_r, wg, wu, wd, sg, su, sd, g, b, tm=256):
    t, d = x.shape
    idx_t, w_t, rank_t, cnt = _router(x, w_r, b_r)

    counts = cnt[:, 0]
    tiles_e = (counts + tm - 1) // tm
    tile_end = jnp.cumsum(tiles_e)
    row_off = (tile_end - tiles_e) * tm
    n_valid = tile_end[-1:]
    n_tiles = t * TOP_K // tm + N_EXPERTS
    pos = row_off[idx_t] + rank_t
    tok = jnp.broadcast_to(jnp.arange(t, dtype=jnp.int32), (TOP_K, t))
    row_token = jnp.zeros((n_tiles * tm,), jnp.int32).at[pos.reshape(-1)].set(tok.reshape(-1))
    tile_ids = jnp.minimum(jnp.arange(n_tiles, dtype=jnp.int32), n_valid - 1)
    tile_expert = jnp.searchsorted(tile_end, tile_ids, side="right").astype(jnp.int32)

    ys = _experts(x, tile_expert, n_valid.astype(jnp.int32), row_token,
                  wg.astype(_BF16), wu.astype(_BF16), wd.astype(_BF16), tm)
    shared = _shared_ffn(xb, sg.astype(_BF16), su.astype(_BF16), sd.astype(_BF16))
    return _combine(pos.reshape(-1), ys, w_t.T, shared, x, g, b)


def _fgate_kernel(x_ref, wt_ref, b_ref, c_ref, carry_ref):
    s = pl.program_id(1)

    @pl.when(s == 0)
    def _():
        carry_ref[...] = jnp.zeros_like(carry_ref)

    ts = x_ref.shape[0]
    z = lax.dot_general(wt_ref[...], x_ref[...], (((1,), (1,)), ((), ())),
                        precision=lax.Precision.HIGHEST,
                        preferred_element_type=_F32) + b_ref[...]
    ls = jnp.minimum(z, 0.0) - jnp.log1p(jnp.exp(-jnp.abs(z)))
    r_i = lax.broadcasted_iota(jnp.int32, (ts, ts), 0)
    c_i = lax.broadcasted_iota(jnp.int32, (ts, ts), 1)
    upto = jnp.where(r_i <= c_i, 1.0, 0.0)
    c = carry_ref[...] + jnp.dot(ls, upto, precision=lax.Precision.HIGHEST,
                                 preferred_element_type=_F32)
    c_ref[...] = c
    carry_ref[...] = c[:, ts - 1:ts]


def _fgate(x3, w_f, b_f, ts=512):
    bsz, s, d = x3.shape
    return pl.pallas_call(
        _fgate_kernel,
        grid=(bsz, s // ts),
        in_specs=[pl.BlockSpec((None, ts, d), lambda bb, i: (bb, i, 0)),
                  pl.BlockSpec((N_HEADS, d), lambda bb, i: (0, 0)),
                  pl.BlockSpec((N_HEADS, 1), lambda bb, i: (0, 0))],
        out_specs=pl.BlockSpec((None, N_HEADS, ts), lambda bb, i: (bb, 0, i)),
        out_shape=jax.ShapeDtypeStruct((bsz, N_HEADS, s), _F32),
        scratch_shapes=[pltpu.VMEM((N_HEADS, 1), _F32)],
        compiler_params=_params(2),
        name="fgate",
    )(x3, w_f.T, b_f.astype(_F32).reshape(N_HEADS, 1))


def _attn_kernel(q_ref, k_ref, v_ref, cq_ref, ck_ref, o_ref, *, tq):
    h = pl.program_id(1)
    qi = pl.program_id(2)
    scale = HEAD_DIM ** -0.5
    q = q_ref[...]
    lane = lax.broadcasted_iota(jnp.int32, cq_ref.shape, 1)
    cq = jnp.sum(jnp.where(lane == h, cq_ref[...], 0.0), axis=1, keepdims=True)

    def step(j, carry, diagonal):
        m, l, acc = carry
        k0 = pl.multiple_of(j * tq, tq)
        kj = k_ref[pl.ds(k0, tq), :]
        vj = v_ref[pl.ds(k0, tq), :]
        s = lax.dot_general(q, kj, (((1,), (1,)), ((), ())),
                            preferred_element_type=_F32) * scale
        s = s + (cq - ck_ref[:, pl.ds(k0, tq)])
        if diagonal:
            r_i = lax.broadcasted_iota(jnp.int32, s.shape, 0)
            c_i = lax.broadcasted_iota(jnp.int32, s.shape, 1)
            s = jnp.where(c_i <= r_i, s, _NEG)
        m_new = jnp.maximum(m, jnp.max(s, axis=1, keepdims=True))
        alpha = jnp.exp(m - m_new)
        p = jnp.exp(s - m_new)
        l = alpha * l + jnp.sum(p, axis=1, keepdims=True)
        acc = alpha * acc + jnp.dot(p.astype(_BF16), vj, preferred_element_type=_F32)
        return m_new, l, acc

    init = (jnp.full((tq, 1), _NEG, _F32), jnp.zeros((tq, 1), _F32),
            jnp.zeros((tq, HEAD_DIM), _F32))
    carry = lax.fori_loop(0, qi, lambda j, c: step(j, c, False), init)
    _, l, acc = step(qi, carry, True)
    o_ref[...] = (acc / l).astype(o_ref.dtype)


def _attention(q3, kv3, c_sh, c_hs, tq=512):
    bsz, s, d = q3.shape
    return pl.pallas_call(
        functools.partial(_attn_kernel, tq=tq),
        grid=(bsz, N_HEADS, s // tq),
        in_specs=[pl.BlockSpec((None, tq, HEAD_DIM), lambda bb, h, i: (bb, i, h)),
                  pl.BlockSpec((None, s, HEAD_DIM), lambda bb, h, i: (bb, 0, h)),
                  pl.BlockSpec((None, s, HEAD_DIM), lambda bb, h, i: (bb, 0, N_HEADS + h)),
                  pl.BlockSpec((None, tq, N_HEADS), lambda bb, h, i: (bb, i, 0)),
                  pl.BlockSpec((None, None, 1, s), lambda bb, h, i: (bb, h, 0, 0))],
        out_specs=pl.BlockSpec((None, tq, HEAD_DIM), lambda bb, h, i: (bb, i, h)),
        out_shape=jax.ShapeDtypeStruct((bsz, s, d), _BF16),
        compiler_params=_params(3),
        name="fox_attention",
    )(q3, kv3, kv3, c_sh, c_hs.reshape(bsz, N_HEADS, 1, s))


def kernel(x, ln_g, ln_b, conv_w_in, conv_w, conv_w_out, kv_w, kv_fb, attn_w_q, attn_w_o,
           router_w, router_b, exp_w_gate, exp_w_up, exp_w_down,
           shared_w_gate, shared_w_up, shared_w_down):
    bsz, seq, d = x.shape
    t = bsz * seq
    xf = x.reshape(t, d)
    xb = xf.astype(_BF16)
    kv3 = c_sh = c_hs = None
    for l in range(DEPTH):
        if l < N_A_LAYERS:
            y = _conv_in(xb, conv_w_in[l].astype(_BF16), conv_w[l], seq)
            h = _mm(y, conv_w_out[l].astype(_BF16), _F32)
        else:
            if l == N_A_LAYERS:
                kv = _mm(xb, kv_w[:, :2 * d].astype(_BF16), _BF16)
                kv3 = kv.reshape(bsz, seq, 2 * d)
                c_hs = _fgate(xf.reshape(bsz, seq, d), kv_w[:, 2 * d:], kv_fb)
                c_sh = jnp.transpose(c_hs, (0, 2, 1))
            j = l - N_A_LAYERS
            q = _mm(xb, attn_w_q[j].astype(_BF16), _BF16)
            o = _attention(q.reshape(bsz, seq, d), kv3, c_sh, c_hs)
            h = _mm(o.reshape(t, d), attn_w_o[j].astype(_BF16), _F32)
        xf, xb = _res_ln(xf, h, ln_g[l, 0], ln_b[l, 0])
        xf, xb = _moe_block(xf, xb, router_w[l], router_b[l],
                            exp_w_gate[l], exp_w_up[l], exp_w_down[l],
                            shared_w_gate[l], shared_w_up[l], shared_w_down[l],
                            ln_g[l, 1], ln_b[l, 1])
    return xf.reshape(bsz, seq, d)
```

```python
import functools
import math

import jax
import jax.numpy as jnp
from jax import lax
from jax.experimental import pallas as pl
from jax.experimental.pallas import tpu as pltpu

D_MODEL = 4096
DEPTH = 2
N_A_LAYERS = DEPTH // 2
HEAD_DIM = 128
N_HEADS = D_MODEL // HEAD_DIM
CONV_WIDTH = 3
N_EXPERTS = 64
N_GROUPS = 8
GROUP_SIZE = N_EXPERTS // N_GROUPS
TOPK_GROUPS = 4
TOP_K = 8
D_EXPERT = 384
ROUTED_SCALE = 2.5
ALPHA = (2.0 * DEPTH) ** 0.25
LN_EPS = 1e-5

_MIB = 1024 * 1024
_VMEM_LIMIT = 56 * _MIB
_NEG = -1e30
_LOG2E = math.log2(math.e)
_ISSUE_UNROLL = 8
_W_CHUNKS = 4

_F32 = jnp.float32
_BF16 = jnp.bfloat16


def _params(n_axes, vmem=_VMEM_LIMIT):
    return pltpu.CompilerParams(dimension_semantics=("arbitrary",) * n_axes,
                                vmem_limit_bytes=vmem)


def _layer_norm_rows(v, g, b):
    mu = jnp.mean(v, axis=-1, keepdims=True)
    xc = v - mu
    var = jnp.mean(xc * xc, axis=-1, keepdims=True)
    return xc * lax.rsqrt(var + LN_EPS) * g + b


def _silu_mul(g, u):
    return g * jax.nn.sigmoid(g) * u


def _mm_kernel(a_ref, w_ref, o_ref, wb_ref):
    @pl.when(pl.program_id(1) == 0)
    def _():
        wb_ref[...] = w_ref[...].astype(_BF16)

    o_ref[...] = jnp.dot(a_ref[...], wb_ref[...],
                         preferred_element_type=_F32).astype(o_ref.dtype)


def _mm(a, w, layer, n, out_dtype, tm=1024, tn=512):
    m, k = a.shape
    return pl.pallas_call(
        _mm_kernel,
        grid=(n // tn, m // tm),
        in_specs=[pl.BlockSpec((tm, k), lambda j, i: (i, 0)),
                  pl.BlockSpec((None, k, tn), lambda j, i: (layer, 0, j))],
        out_specs=pl.BlockSpec((tm, tn), lambda j, i: (i, j)),
        out_shape=jax.ShapeDtypeStruct((m, n), out_dtype),
        scratch_shapes=[pltpu.VMEM((k, tn), _BF16)],
        compiler_params=_params(2),
        name="dense_mm",
    )(a, w)


def _conv_in_kernel(a_ref, wb_ref, wc_ref, wu_ref, cw_ref, y_ref, wbf_ref, carry_ref,
                    *, tiles_per_seq):
    i = pl.program_id(1)

    @pl.when(i == 0)
    def _():
        wbf_ref[0] = wb_ref[...].astype(_BF16)
        wbf_ref[1] = wc_ref[...].astype(_BF16)
        wbf_ref[2] = wu_ref[...].astype(_BF16)

    @pl.when(i % tiles_per_seq == 0)
    def _():
        carry_ref[...] = jnp.zeros_like(carry_ref)

    a = a_ref[...]
    gate_out = jnp.dot(a, wbf_ref[0], preferred_element_type=_F32)
    gate_in = jnp.dot(a, wbf_ref[1], preferred_element_type=_F32)
    u = jnp.dot(a, wbf_ref[2], preferred_element_type=_F32)
    z = gate_in * u
    tm = z.shape[0]
    row = lax.broadcasted_iota(jnp.int32, z.shape, 0)
    p1 = carry_ref[7:8, :]
    p2 = carry_ref[6:7, :]
    z1 = jnp.where(row == 0, p1, pltpu.roll(z, 1, 0))
    z2 = jnp.where(row == 0, p2, jnp.where(row == 1, p1, pltpu.roll(z, 2, 0)))
    cw = cw_ref[...]
    conv = cw[0:1, :] * z2 + cw[1:2, :] * z1 + cw[2:3, :] * z
    y_ref[...] = (gate_out * conv).astype(y_ref.dtype)
    carry_ref[...] = z[tm - 8:tm, :]


def _conv_in(a, w_in, conv_w, layer, seq, tm=1024, tn=256):
    m, k = a.shape
    d = w_in.shape[2] // 3
    nb = d // tn
    kern = functools.partial(_conv_in_kernel, tiles_per_seq=seq // tm)
    return pl.pallas_call(
        kern,
        grid=(nb, m // tm),
        in_specs=[pl.BlockSpec((tm, k), lambda j, i: (i, 0)),
                  pl.BlockSpec((None, k, tn), lambda j, i: (layer, 0, j)),
                  pl.BlockSpec((None, k, tn), lambda j, i: (layer, 0, j + nb)),
                  pl.BlockSpec((None, k, tn), lambda j, i: (layer, 0, j + 2 * nb)),
                  pl.BlockSpec((None, CONV_WIDTH, tn), lambda j, i: (layer, 0, j))],
        out_specs=pl.BlockSpec((tm, tn), lambda j, i: (i, j)),
        out_shape=jax.ShapeDtypeStruct((m, d), _BF16),
        scratch_shapes=[pltpu.VMEM((3, k, tn), _BF16), pltpu.VMEM((8, tn), _F32)],
        compiler_params=_params(2),
        name="conv_in",
    )(a, w_in, w_in, w_in, conv_w)


def _res_ln_kernel(x_ref, h_ref, g_ref, b_ref, o_ref, ob_ref):
    v = ALPHA * x_ref[...] + h_ref[...]
    y = _layer_norm_rows(v, g_ref[...], b_ref[...])
    o_ref[...] = y
    ob_ref[...] = y.astype(_BF16)


def _res_ln(x, h, g, b, tm=256):
    m, d = x.shape
    row = pl.BlockSpec((tm, d), lambda i: (i, 0))
    vec = pl.BlockSpec((1, d), lambda i: (0, 0))
    return pl.pallas_call(
        _res_ln_kernel,
        grid=(m // tm,),
        in_specs=[row, row, vec, vec],
        out_specs=[row, row],
        out_shape=[jax.ShapeDtypeStruct((m, d), _F32), jax.ShapeDtypeStruct((m, d), _BF16)],
        compiler_params=_params(1),
        name="res_ln",
    )(x, h, g.reshape(1, d), b.reshape(1, d))


def _first_index_of_max(vals, axis, size):
    mx = jnp.max(vals, axis=axis, keepdims=True)
    idx = lax.broadcasted_iota(jnp.int32, vals.shape, axis)
    first = jnp.min(jnp.where(vals == mx, idx, size), axis=axis, keepdims=True)
    return mx, idx == first


def _router_kernel(x_ref, wt_ref, b_ref, idx_ref, w_ref, rank_ref, cnt_ref, carry_ref):
    i = pl.program_id(0)

    @pl.when(i == 0)
    def _():
        carry_ref[...] = jnp.zeros_like(carry_ref)

    tm = x_ref.shape[0]
    logits = lax.dot_general(wt_ref[...], x_ref[...], (((1,), (1,)), ((), ())),
                             precision=lax.Precision.HIGHEST,
                             preferred_element_type=_F32)
    scores = jax.nn.sigmoid(logits)
    choice = scores + b_ref[...]
    c3 = choice.reshape(N_GROUPS, GROUP_SIZE, tm)
    m1, hit1 = _first_index_of_max(c3, 1, GROUP_SIZE)
    m2 = jnp.max(jnp.where(hit1, _NEG, c3), axis=1, keepdims=True)
    cur = (m1 + m2).reshape(N_GROUPS, tm)
    sel = jnp.zeros((N_GROUPS, tm), _F32)
    for _ in range(TOPK_GROUPS):
        _, hit = _first_index_of_max(cur, 0, N_GROUPS)
        sel = jnp.where(hit, 1.0, sel)
        cur = jnp.where(hit, _NEG, cur)
    exp_mask = jnp.broadcast_to(sel.reshape(N_GROUPS, 1, tm),
                                (N_GROUPS, GROUP_SIZE, tm)).reshape(N_EXPERTS, tm)
    masked = jnp.where(exp_mask > 0.0, choice, _NEG)

    eidx = lax.broadcasted_iota(jnp.int32, (N_EXPERTS, tm), 0)
    hits, idxs, ws = [], [], []
    member = jnp.zeros((N_EXPERTS, tm), _F32)
    for _ in range(TOP_K):
        _, hit = _first_index_of_max(masked, 0, N_EXPERTS)
        hits.append(hit)
        idxs.append(jnp.sum(jnp.where(hit, eidx, 0), axis=0, keepdims=True))
        ws.append(jnp.sum(jnp.where(hit, scores, 0.0), axis=0, keepdims=True))
        member = jnp.where(hit, 1.0, member)
        masked = jnp.where(hit, _NEG, masked)
    w = jnp.concatenate(ws, axis=0)
    w = w / jnp.sum(w, axis=0, keepdims=True) * ROUTED_SCALE
    idx_ref[...] = jnp.concatenate(idxs, axis=0)
    w_ref[...] = w

    r_i = lax.broadcasted_iota(jnp.int32, (tm, tm), 0)
    c_i = lax.broadcasted_iota(jnp.int32, (tm, tm), 1)
    before = jnp.where(r_i < c_i, 1.0, 0.0).astype(_BF16)
    prefix = jnp.dot(member.astype(_BF16), before, preferred_element_type=_F32)
    base = carry_ref[...] + prefix
    ranks = [jnp.sum(jnp.where(h, base, 0.0), axis=0, keepdims=True) for h in hits]
    rank_ref[...] = jnp.concatenate(ranks, axis=0).astype(jnp.int32)
    total = carry_ref[...] + jnp.sum(member, axis=1, keepdims=True)
    carry_ref[...] = total
    cnt_ref[...] = total.astype(jnp.int32)


def _router(x, w_r, b_r, tm=512):
    t, d = x.shape
    kt = pl.BlockSpec((TOP_K, tm), lambda i: (0, i))
    return pl.pallas_call(
        _router_kernel,
        grid=(t // tm,),
        in_specs=[pl.BlockSpec((tm, d), lambda i: (i, 0)),
                  pl.BlockSpec((N_EXPERTS, d), lambda i: (0, 0)),
                  pl.BlockSpec((N_EXPERTS, 1), lambda i: (0, 0))],
        out_specs=[kt, kt, kt, pl.BlockSpec((N_EXPERTS, 1), lambda i: (0, 0))],
        out_shape=[jax.ShapeDtypeStruct((TOP_K, t), jnp.int32),
                   jax.ShapeDtypeStruct((TOP_K, t), _F32),
                   jax.ShapeDtypeStruct((TOP_K, t), jnp.int32),
                   jax.ShapeDtypeStruct((N_EXPERTS, 1), jnp.int32)],
        scratch_shapes=[pltpu.VMEM((N_EXPERTS, 1), _F32)],
        compiler_params=_params(1),
        name="router",
    )(x, w_r.T, b_r.astype(_F32).reshape(N_EXPERTS, 1))


def _experts_kernel(te_ref, nv_ref, nxt_ref, clo_ref, chi_ref, par_ref, tok_ref,
                    x_hbm, wg_hbm, wu_hbm, wd_hbm, y_ref,
                    xbuf, wgu_b, wd_b, stg_a, stg_d, gsem, sem_a, sem_d, *, tm, layer):
    i = pl.program_id(0)
    n_valid = nv_ref[0]
    d = x_hbm.shape[1]
    ka = d // _W_CHUNKS
    kd = D_EXPERT // _W_CHUNKS
    n_chunks = 3 * _W_CHUNKS

    def chunk_copy(c, e, w_hbm, rows, stg, sem):
        s2 = c % 2
        r0 = pl.multiple_of((c % _W_CHUNKS) * rows, rows)
        return pltpu.make_async_copy(w_hbm.at[layer, e, pl.ds(r0, rows), :], stg.at[s2], sem.at[s2])

    def start_chunk(c, e):
        @pl.when(c < _W_CHUNKS)
        def _():
            chunk_copy(c, e, wg_hbm, ka, stg_a, sem_a).start()

        @pl.when((c >= _W_CHUNKS) & (c < 2 * _W_CHUNKS))
        def _():
            chunk_copy(c, e, wu_hbm, ka, stg_a, sem_a).start()

        @pl.when(c >= 2 * _W_CHUNKS)
        def _():
            chunk_copy(c, e, wd_hbm, kd, stg_d, sem_d).start()

    def finish_chunk(c, e, slot):
        s2 = c % 2
        ra = pl.multiple_of((c % _W_CHUNKS) * ka, ka)
        rd = pl.multiple_of((c % _W_CHUNKS) * kd, kd)

        @pl.when(c < _W_CHUNKS)
        def _():
            chunk_copy(c, e, wg_hbm, ka, stg_a, sem_a).wait()
            wgu_b[slot, pl.ds(ra, ka), 0:D_EXPERT] = stg_a[s2].astype(_BF16)

        @pl.when((c >= _W_CHUNKS) & (c < 2 * _W_CHUNKS))
        def _():
            chunk_copy(c, e, wu_hbm, ka, stg_a, sem_a).wait()
            wgu_b[slot, pl.ds(ra, ka), D_EXPERT:2 * D_EXPERT] = stg_a[s2].astype(_BF16)

        @pl.when(c >= 2 * _W_CHUNKS)
        def _():
            chunk_copy(c, e, wd_hbm, kd, stg_d, sem_d).wait()
            wd_b[slot, pl.ds(rd, kd), :] = stg_d[s2].astype(_BF16)

    def load_chunks(lo, hi, e, slot):
        def body(c, carry):
            finish_chunk(c, e, slot)

            @pl.when(c + 2 < n_chunks)
            def _():
                start_chunk(c + 2, e)
            return carry
        lax.fori_loop(lo, hi, body, 0)

    def start_gather(tile, slot):
        base = tile * tm

        def body(r8, c):
            for u in range(_ISSUE_UNROLL):
                r = r8 * _ISSUE_UNROLL + u
                tok = tok_ref[base + r]
                pltpu.make_async_copy(x_hbm.at[pl.ds(tok, 1), :],
                                      xbuf.at[slot, pl.ds(r, 1), :], gsem.at[slot]).start()
            return c
        lax.fori_loop(0, tm // _ISSUE_UNROLL, body, 0)

    @pl.when(i == 0)
    def _():
        start_gather(0, 0)
        start_chunk(0, te_ref[0])
        start_chunk(1, te_ref[0])
        load_chunks(0, n_chunks, te_ref[0], par_ref[0])

    @pl.when(i + 1 < n_valid)
    def _():
        start_gather(i + 1, (i + 1) % 2)

    @pl.when(i < n_valid)
    def _():
        slot = par_ref[i]
        nxt = nxt_ref[i]

        @pl.when(nxt >= 0)
        def _():
            @pl.when(clo_ref[i] == 0)
            def _():
                start_chunk(0, nxt)
                start_chunk(1, nxt)
            load_chunks(clo_ref[i], chi_ref[i], nxt, 1 - slot)

        gslot = i % 2
        pltpu.make_async_copy(x_hbm.at[pl.ds(0, tm), :], xbuf.at[gslot], gsem.at[gslot]).wait()
        a = xbuf[gslot].astype(_BF16)
        gu = jnp.dot(a, wgu_b[slot], preferred_element_type=_F32)
        h = _silu_mul(gu[:, :D_EXPERT], gu[:, D_EXPERT:]).astype(_BF16)
        y_ref[...] = jnp.dot(h, wd_b[slot], preferred_element_type=_F32)

    @pl.when(i >= n_valid)
    def _():
        y_ref[...] = jnp.zeros_like(y_ref)


def _experts(x, meta, wg, wu, wd, layer, tm):
    t, d = x.shape
    n_tiles = meta[0].shape[0]
    ka = d // _W_CHUNKS
    kd = D_EXPERT // _W_CHUNKS
    any_spec = pl.BlockSpec(memory_space=pl.ANY)
    grid_spec = pltpu.PrefetchScalarGridSpec(
        num_scalar_prefetch=len(meta),
        grid=(n_tiles,),
        in_specs=[any_spec, any_spec, any_spec, any_spec],
        out_specs=pl.BlockSpec((tm, d), lambda i, *_: (i, 0)),
        scratch_shapes=[pltpu.VMEM((2, tm, d), _F32),
                        pltpu.VMEM((2, d, 2 * D_EXPERT), _BF16),
                        pltpu.VMEM((2, D_EXPERT, d), _BF16),
                        pltpu.VMEM((2, ka, D_EXPERT), _F32),
                        pltpu.VMEM((2, kd, d), _F32),
                        pltpu.SemaphoreType.DMA((2,)),
                        pltpu.SemaphoreType.DMA((2,)),
                        pltpu.SemaphoreType.DMA((2,))],
    )
    return pl.pallas_call(
        functools.partial(_experts_kernel, tm=tm, layer=layer),
        grid_spec=grid_spec,
        out_shape=jax.ShapeDtypeStruct((n_tiles * tm, d), _F32),
        compiler_params=_params(1),
        name="experts",
    )(*meta, x, wg, wu, wd)


def _ffn_kernel(a_ref, wg_ref, wu_ref, wd_ref, o_ref):
    a = a_ref[...]
    g = jnp.dot(a, wg_ref[...], preferred_element_type=_F32)
    u = jnp.dot(a, wu_ref[...], preferred_element_type=_F32)
    h = _silu_mul(g, u).astype(_BF16)
    o_ref[...] = jnp.dot(h, wd_ref[...], preferred_element_type=_F32)


def _shared_ffn(a, wg, wu, wd, tm=512):
    m, d = a.shape
    f = wg.shape[1]
    return pl.pallas_call(
        _ffn_kernel,
        grid=(m // tm,),
        in_specs=[pl.BlockSpec((tm, d), lambda i: (i, 0)),
                  pl.BlockSpec((d, f), lambda i: (0, 0)),
                  pl.BlockSpec((d, f), lambda i: (0, 0)),
                  pl.BlockSpec((f, d), lambda i: (0, 0))],
        out_specs=pl.BlockSpec((tm, d), lambda i: (i, 0)),
        out_shape=jax.ShapeDtypeStruct((m, d), _F32),
        compiler_params=_params(1),
        name="shared_ffn",
    )(a, wg, wu, wd)


def _combine_kernel(pos_ref, y_hbm, w_ref, sh_ref, x_ref, g_ref, b_ref, o_ref, ob_ref,
                    ybuf, sem, *, tc, n_tok):
    i = pl.program_id(0)
    n_steps = pl.num_programs(0)

    def start_gather(step, slot):
        for k in range(TOP_K):
            base = k * n_tok + step * tc

            def body(r8, c, k=k, base=base):
                for u in range(_ISSUE_UNROLL):
                    r = r8 * _ISSUE_UNROLL + u
                    p = pos_ref[base + r]
                    pltpu.make_async_copy(y_hbm.at[pl.ds(p, 1), :],
                                          ybuf.at[slot, k, pl.ds(r, 1), :], sem.at[slot]).start()
                return c
            lax.fori_loop(0, tc // _ISSUE_UNROLL, body, 0)

    @pl.when(i == 0)
    def _():
        start_gather(0, 0)

    @pl.when(i + 1 < n_steps)
    def _():
        start_gather(i + 1, (i + 1) % 2)

    slot = i % 2
    for k in range(TOP_K):
        pltpu.make_async_copy(y_hbm.at[pl.ds(0, tc), :], ybuf.at[slot, k], sem.at[slot]).wait()
    w = w_ref[...]
    acc = sh_ref[...]
    for k in range(TOP_K):
        acc = acc + w[:, k:k + 1] * ybuf[slot, k]
    v = ALPHA * x_ref[...] + acc
    y = _layer_norm_rows(v, g_ref[...], b_ref[...])
    o_ref[...] = y
    ob_ref[...] = y.astype(_BF16)


def _combine(pos_flat, ys, w_tok, shared, x, g, b, tc=64):
    t, d = x.shape
    row = pl.BlockSpec((tc, d), lambda i, pos: (i, 0))
    vec = pl.BlockSpec((1, d), lambda i, pos: (0, 0))
    grid_spec = pltpu.PrefetchScalarGridSpec(
        num_scalar_prefetch=1,
        grid=(t // tc,),
        in_specs=[pl.BlockSpec(memory_space=pl.ANY),
                  pl.BlockSpec((tc, TOP_K), lambda i, pos: (i, 0)),
                  row, row, vec, vec],
        out_specs=[row, row],
        scratch_shapes=[pltpu.VMEM((2, TOP_K, tc, d), _F32), pltpu.SemaphoreType.DMA((2,))],
    )
    return pl.pallas_call(
        functools.partial(_combine_kernel, tc=tc, n_tok=t),
        grid_spec=grid_spec,
        out_shape=[jax.ShapeDtypeStruct((t, d), _F32), jax.ShapeDtypeStruct((t, d), _BF16)],
        compiler_params=_params(1),
        name="combine",
    )(pos_flat, ys, w_tok, shared, x, g.reshape(1, d), b.reshape(1, d))


def _moe_block(x, xb, layer, w_r, b_r, wg, wu, wd, sg, su, sd, g, b, tm=256):
    t, d = x.shape
    idx_t, w_t, rank_t, cnt = _router(x, w_r, b_r)

    counts = cnt[:, 0]
    tiles_e = (counts + tm - 1) // tm
    tile_end = jnp.cumsum(tiles_e)
    tile_start = tile_end - tiles_e
    row_off = tile_start * tm
    n_valid = tile_end[-1:]
    n_tiles = t * TOP_K // tm + N_EXPERTS
    e_ids = jnp.arange(N_EXPERTS, dtype=jnp.int32)
    pos = jnp.sum(jnp.where(idx_t[:, :, None] == e_ids, row_off, 0), axis=-1) + rank_t
    tok = jnp.broadcast_to(jnp.arange(t, dtype=jnp.int32), (TOP_K, t))
    row_token = jnp.zeros((n_tiles * tm,), jnp.int32).at[pos.reshape(-1)].set(
        tok.reshape(-1), unique_indices=True, indices_are_sorted=False)
    tile_ids = jnp.minimum(jnp.arange(n_tiles, dtype=jnp.int32), n_valid - 1)
    onehot_te = (tile_end[None, :] <= tile_ids[:, None]).astype(jnp.int32)
    tile_expert = jnp.sum(onehot_te, axis=1)

    has = tiles_e > 0
    later = (e_ids[None, :] > e_ids[:, None]) & has[None, :]
    nxt_e = jnp.min(jnp.where(later, e_ids[None, :], N_EXPERTS), axis=1)
    nxt_e = jnp.where(nxt_e == N_EXPERTS, -1, nxt_e)
    par_e = (jnp.cumsum(has.astype(jnp.int32)) - has.astype(jnp.int32)) % 2

    def per_tile(v):
        return jnp.sum(jnp.where(tile_expert[:, None] == e_ids[None, :], v[None, :], 0), axis=1)
    n_run = jnp.maximum(per_tile(tiles_e), 1)
    p_run = tile_ids - per_tile(tile_start)
    n_chunks = 3 * _W_CHUNKS
    meta = (tile_expert, n_valid.astype(jnp.int32), per_tile(nxt_e),
            (n_chunks * p_run) // n_run, (n_chunks * (p_run + 1)) // n_run, per_tile(par_e),
            row_token)
    meta = tuple(m.astype(jnp.int32) for m in meta)

    ys = _experts(x, meta, wg, wu, wd, layer, tm)
    shared = _shared_ffn(xb, sg.astype(_BF16), su.astype(_BF16), sd.astype(_BF16))
    return _combine(pos.reshape(-1), ys, w_t.T, shared, x, g, b)


def _fgate_kernel(x_ref, wt_ref, b_ref, c_ref, carry_ref):
    s = pl.program_id(1)

    @pl.when(s == 0)
    def _():
        carry_ref[...] = jnp.zeros_like(carry_ref)

    ts = x_ref.shape[0]
    z = lax.dot_general(wt_ref[...], x_ref[...], (((1,), (1,)), ((), ())),
                        precision=lax.Precision.HIGHEST,
                        preferred_element_type=_F32) + b_ref[...]
    ls = jnp.minimum(z, 0.0) - jnp.log1p(jnp.exp(-jnp.abs(z)))
    r_i = lax.broadcasted_iota(jnp.int32, (ts, ts), 0)
    c_i = lax.broadcasted_iota(jnp.int32, (ts, ts), 1)
    upto = jnp.where(r_i <= c_i, 1.0, 0.0)
    c = carry_ref[...] + jnp.dot(ls, upto, precision=lax.Precision.HIGHEST,
                                 preferred_element_type=_F32)
    c_ref[...] = c
    carry_ref[...] = c[:, ts - 1:ts]


def _fgate(x3, w_f, b_f, ts=512):
    bsz, s, d = x3.shape
    return pl.pallas_call(
        _fgate_kernel,
        grid=(bsz, s // ts),
        in_specs=[pl.BlockSpec((None, ts, d), lambda bb, i: (bb, i, 0)),
                  pl.BlockSpec((N_HEADS, d), lambda bb, i: (0, 0)),
                  pl.BlockSpec((N_HEADS, 1), lambda bb, i: (0, 0))],
        out_specs=pl.BlockSpec((None, N_HEADS, ts), lambda bb, i: (bb, 0, i)),
        out_shape=jax.ShapeDtypeStruct((bsz, N_HEADS, s), _F32),
        scratch_shapes=[pltpu.VMEM((N_HEADS, 1), _F32)],
        compiler_params=_params(2),
        name="fgate",
    )(x3, w_f.T, b_f.astype(_F32).reshape(N_HEADS, 1))


def _attn_kernel(q_ref, k_ref, v_ref, cq_ref, ck_ref, o_ref, *, tq):
    h = pl.program_id(1)
    qi = pl.program_id(2)
    qk_scale = HEAD_DIM ** -0.5 * _LOG2E
    q = q_ref[...]
    lane = lax.broadcasted_iota(jnp.int32, cq_ref.shape, 1)
    cq = _LOG2E * jnp.sum(jnp.where(lane == h, cq_ref[...], 0.0), axis=1, keepdims=True)

    def step(j, carry, diagonal):
        m, l, acc = carry
        k0 = pl.multiple_of(j * tq, tq)
        kj = k_ref[pl.ds(k0, tq), :]
        vj = v_ref[pl.ds(k0, tq), :]
        ck = _LOG2E * ck_ref[:, pl.ds(k0, tq)]
        s = lax.dot_general(q, kj, (((1,), (1,)), ((), ())),
                            preferred_element_type=_F32) * qk_scale - ck
        if diagonal:
            r_i = lax.broadcasted_iota(jnp.int32, s.shape, 0)
            c_i = lax.broadcasted_iota(jnp.int32, s.shape, 1)
            s = jnp.where(c_i <= r_i, s, _NEG)
        m_new = jnp.maximum(m, jnp.max(s, axis=1, keepdims=True) + cq)
        alpha = jnp.exp2(m - m_new)
        p = jnp.exp2(s - (m_new - cq))
        l = alpha * l + jnp.sum(p, axis=1, keepdims=True)
        acc = alpha * acc + jnp.dot(p.astype(_BF16), vj, preferred_element_type=_F32)
        return m_new, l, acc

    init = (jnp.full((tq, 1), _NEG, _F32), jnp.zeros((tq, 1), _F32),
            jnp.zeros((tq, HEAD_DIM), _F32))
    carry = lax.fori_loop(0, qi, lambda j, c: step(j, c, False), init)
    _, l, acc = step(qi, carry, True)
    o_ref[...] = (acc / l).astype(o_ref.dtype)


def _attention(q3, kv3, c_sh, c_hs, tq=1024):
    bsz, s, d = q3.shape
    return pl.pallas_call(
        functools.partial(_attn_kernel, tq=tq),
        grid=(bsz, N_HEADS, s // tq),
        in_specs=[pl.BlockSpec((None, tq, HEAD_DIM), lambda bb, h, i: (bb, i, h)),
                  pl.BlockSpec((None, s, HEAD_DIM), lambda bb, h, i: (bb, 0, h)),
                  pl.BlockSpec((None, s, HEAD_DIM), lambda bb, h, i: (bb, 0, N_HEADS + h)),
                  pl.BlockSpec((None, tq, N_HEADS), lambda bb, h, i: (bb, i, 0)),
                  pl.BlockSpec((None, None, 1, s), lambda bb, h, i: (bb, h, 0, 0))],
        out_specs=pl.BlockSpec((None, tq, HEAD_DIM), lambda bb, h, i: (bb, i, h)),
        out_shape=jax.ShapeDtypeStruct((bsz, s, d), _BF16),
        compiler_params=_params(3),
        name="fox_attention",
    )(q3, kv3, kv3, c_sh, c_hs.reshape(bsz, N_HEADS, 1, s))


def kernel(x, ln_g, ln_b, conv_w_in, conv_w, conv_w_out, kv_w, kv_fb, attn_w_q, attn_w_o,
           router_w, router_b, exp_w_gate, exp_w_up, exp_w_down,
           shared_w_gate, shared_w_up, shared_w_down):
    bsz, seq, d = x.shape
    t = bsz * seq
    xf = x.reshape(t, d)
    xb = xf.astype(_BF16)
    kv3 = c_sh = c_hs = None
    for l in range(DEPTH):
        if l < N_A_LAYERS:
            y = _conv_in(xb, conv_w_in, conv_w, l, seq)
            h = _mm(y, conv_w_out, l, d, _F32)
        else:
            j = l - N_A_LAYERS
            if l == N_A_LAYERS:
                kv = _mm(xb, kv_w.reshape(1, d, kv_w.shape[1]), 0, 2 * d, _BF16)
                kv3 = kv.reshape(bsz, seq, 2 * d)
                c_hs = _fgate(xf.reshape(bsz, seq, d), kv_w[:, 2 * d:], kv_fb)
                c_sh = jnp.transpose(c_hs, (0, 2, 1))
            q = _mm(xb, attn_w_q, j, d, _BF16)
            o = _attention(q.reshape(bsz, seq, d), kv3, c_sh, c_hs)
            h = _mm(o.reshape(t, d), attn_w_o, j, d, _F32)
        xf, xb = _res_ln(xf, h, ln_g[l, 0], ln_b[l, 0])
        xf, xb = _moe_block(xf, xb, l, router_w[l], router_b[l],
                            exp_w_gate, exp_w_up, exp_w_down,
                            shared_w_gate[l], shared_w_up[l], shared_w_down[l],
                            ln_g[l, 1], ln_b[l, 1])
    return xf.reshape(bsz, seq, d)
```

```python
import functools
import math

import jax
import jax.numpy as jnp
from jax import lax
from jax.experimental import pallas as pl
from jax.experimental.pallas import tpu as pltpu

D_MODEL = 4096
DEPTH = 2
N_A_LAYERS = DEPTH // 2
HEAD_DIM = 128
N_HEADS = D_MODEL // HEAD_DIM
CONV_WIDTH = 3
N_EXPERTS = 64
N_GROUPS = 8
GROUP_SIZE = N_EXPERTS // N_GROUPS
TOPK_GROUPS = 4
TOP_K = 8
D_EXPERT = 384
ROUTED_SCALE = 2.5
ALPHA = (2.0 * DEPTH) ** 0.25
LN_EPS = 1e-5

_MIB = 1024 * 1024
_VMEM_LIMIT = 56 * _MIB
_NEG = -1e30
_LOG2E = math.log2(math.e)
_ISSUE_UNROLL = 8
_W_CHUNKS = 4
_LANES = 128
_HEADS_PER_STEP = 2

_F32 = jnp.float32
_BF16 = jnp.bfloat16


def _params(n_axes, vmem=_VMEM_LIMIT):
    return pltpu.CompilerParams(dimension_semantics=("arbitrary",) * n_axes,
                                vmem_limit_bytes=vmem)


def _layer_norm_rows(v, g, b):
    mu = jnp.mean(v, axis=-1, keepdims=True)
    xc = v - mu
    var = jnp.mean(xc * xc, axis=-1, keepdims=True)
    return xc * lax.rsqrt(var + LN_EPS) * g + b


def _silu_mul(g, u):
    return g * jax.nn.sigmoid(g) * u


def _to_slab(v):
    return pltpu.einshape("m(hd)->mhd", v, h=v.shape[1] // _LANES, d=_LANES)


def _from_slab(v):
    return pltpu.einshape("mhd->m(hd)", v)


def _mm_kernel(a_ref, w_ref, o_ref, wb_ref):
    @pl.when(pl.program_id(1) == 0)
    def _():
        wb_ref[...] = w_ref[...].astype(_BF16)

    o_ref[...] = jnp.dot(a_ref[...], wb_ref[...],
                         preferred_element_type=_F32).astype(o_ref.dtype)


def _mm(a, w, layer, n, out_dtype, tm=1024, tn=512):
    m, k = a.shape
    return pl.pallas_call(
        _mm_kernel,
        grid=(n // tn, m // tm),
        in_specs=[pl.BlockSpec((tm, k), lambda j, i: (i, 0)),
                  pl.BlockSpec((None, k, tn), lambda j, i: (layer, 0, j))],
        out_specs=pl.BlockSpec((tm, tn), lambda j, i: (i, j)),
        out_shape=jax.ShapeDtypeStruct((m, n), out_dtype),
        scratch_shapes=[pltpu.VMEM((k, tn), _BF16)],
        compiler_params=_params(2),
        name="dense_mm",
    )(a, w)


def _conv_in_kernel(a_ref, wb_ref, wc_ref, wu_ref, cw_ref, y_ref, wbf_ref, carry_ref,
                    *, tiles_per_seq):
    i = pl.program_id(1)

    @pl.when(i == 0)
    def _():
        wbf_ref[0] = wb_ref[...].astype(_BF16)
        wbf_ref[1] = wc_ref[...].astype(_BF16)
        wbf_ref[2] = wu_ref[...].astype(_BF16)

    @pl.when(i % tiles_per_seq == 0)
    def _():
        carry_ref[...] = jnp.zeros_like(carry_ref)

    a = a_ref[...]
    gate_out = jnp.dot(a, wbf_ref[0], preferred_element_type=_F32)
    gate_in = jnp.dot(a, wbf_ref[1], preferred_element_type=_F32)
    u = jnp.dot(a, wbf_ref[2], preferred_element_type=_F32)
    z = gate_in * u
    tm = z.shape[0]
    row = lax.broadcasted_iota(jnp.int32, z.shape, 0)
    p1 = carry_ref[7:8, :]
    p2 = carry_ref[6:7, :]
    z1 = jnp.where(row == 0, p1, pltpu.roll(z, 1, 0))
    z2 = jnp.where(row == 0, p2, jnp.where(row == 1, p1, pltpu.roll(z, 2, 0)))
    cw = cw_ref[...]
    conv = cw[0:1, :] * z2 + cw[1:2, :] * z1 + cw[2:3, :] * z
    y_ref[...] = (gate_out * conv).astype(y_ref.dtype)
    carry_ref[...] = z[tm - 8:tm, :]


def _conv_in(a, w_in, conv_w, layer, seq, tm=1024, tn=256):
    m, k = a.shape
    d = w_in.shape[2] // 3
    nb = d // tn
    kern = functools.partial(_conv_in_kernel, tiles_per_seq=seq // tm)
    return pl.pallas_call(
        kern,
        grid=(nb, m // tm),
        in_specs=[pl.BlockSpec((tm, k), lambda j, i: (i, 0)),
                  pl.BlockSpec((None, k, tn), lambda j, i: (layer, 0, j)),
                  pl.BlockSpec((None, k, tn), lambda j, i: (layer, 0, j + nb)),
                  pl.BlockSpec((None, k, tn), lambda j, i: (layer, 0, j + 2 * nb)),
                  pl.BlockSpec((None, CONV_WIDTH, tn), lambda j, i: (layer, 0, j))],
        out_specs=pl.BlockSpec((tm, tn), lambda j, i: (i, j)),
        out_shape=jax.ShapeDtypeStruct((m, d), _BF16),
        scratch_shapes=[pltpu.VMEM((3, k, tn), _BF16), pltpu.VMEM((8, tn), _F32)],
        compiler_params=_params(2),
        name="conv_in",
    )(a, w_in, w_in, w_in, conv_w)


def _res_ln_kernel(x_ref, h_ref, g_ref, b_ref, o_ref, ob_ref, os_ref):
    v = ALPHA * x_ref[...] + h_ref[...]
    y = _layer_norm_rows(v, g_ref[...], b_ref[...])
    yb = y.astype(_BF16)
    o_ref[...] = y
    ob_ref[...] = yb
    os_ref[...] = _to_slab(yb)


def _res_ln(x, h, g, b, tm=256):
    m, d = x.shape
    row = pl.BlockSpec((tm, d), lambda i: (i, 0))
    vec = pl.BlockSpec((1, d), lambda i: (0, 0))
    return pl.pallas_call(
        _res_ln_kernel,
        grid=(m // tm,),
        in_specs=[row, row, vec, vec],
        out_specs=[row, row, pl.BlockSpec((tm, d // _LANES, _LANES), lambda i: (i, 0, 0))],
        out_shape=[jax.ShapeDtypeStruct((m, d), _F32), jax.ShapeDtypeStruct((m, d), _BF16),
                   jax.ShapeDtypeStruct((m, d // _LANES, _LANES), _BF16)],
        compiler_params=_params(1),
        name="res_ln",
    )(x, h, g.reshape(1, d), b.reshape(1, d))


def _first_index_of_max(vals, axis, size):
    mx = jnp.max(vals, axis=axis, keepdims=True)
    idx = lax.broadcasted_iota(jnp.int32, vals.shape, axis)
    first = jnp.min(jnp.where(vals == mx, idx, size), axis=axis, keepdims=True)
    return mx, idx == first


def _router_kernel(x_ref, wt_ref, b_ref, idx_ref, w_ref, rank_ref, cnt_ref, carry_ref):
    i = pl.program_id(0)

    @pl.when(i == 0)
    def _():
        carry_ref[...] = jnp.zeros_like(carry_ref)

    tm = x_ref.shape[0]
    logits = lax.dot_general(wt_ref[...], x_ref[...], (((1,), (1,)), ((), ())),
                             precision=lax.Precision.HIGHEST,
                             preferred_element_type=_F32)
    scores = jax.nn.sigmoid(logits)
    choice = scores + b_ref[...]
    c3 = choice.reshape(N_GROUPS, GROUP_SIZE, tm)
    m1, hit1 = _first_index_of_max(c3, 1, GROUP_SIZE)
    m2 = jnp.max(jnp.where(hit1, _NEG, c3), axis=1, keepdims=True)
    cur = (m1 + m2).reshape(N_GROUPS, tm)
    sel = jnp.zeros((N_GROUPS, tm), _F32)
    for _ in range(TOPK_GROUPS):
        _, hit = _first_index_of_max(cur, 0, N_GROUPS)
        sel = jnp.where(hit, 1.0, sel)
        cur = jnp.where(hit, _NEG, cur)
    exp_mask = jnp.broadcast_to(sel.reshape(N_GROUPS, 1, tm),
                                (N_GROUPS, GROUP_SIZE, tm)).reshape(N_EXPERTS, tm)
    masked = jnp.where(exp_mask > 0.0, choice, _NEG)

    eidx = lax.broadcasted_iota(jnp.int32, (N_EXPERTS, tm), 0)
    hits, idxs, ws = [], [], []
    member = jnp.zeros((N_EXPERTS, tm), _F32)
    for _ in range(TOP_K):
        _, hit = _first_index_of_max(masked, 0, N_EXPERTS)
        hits.append(hit)
        idxs.append(jnp.sum(jnp.where(hit, eidx, 0), axis=0, keepdims=True))
        ws.append(jnp.sum(jnp.where(hit, scores, 0.0), axis=0, keepdims=True))
        member = jnp.where(hit, 1.0, member)
        masked = jnp.where(hit, _NEG, masked)
    w = jnp.concatenate(ws, axis=0)
    w = w / jnp.sum(w, axis=0, keepdims=True) * ROUTED_SCALE
    idx_ref[...] = jnp.concatenate(idxs, axis=0)
    w_ref[...] = w

    r_i = lax.broadcasted_iota(jnp.int32, (tm, tm), 0)
    c_i = lax.broadcasted_iota(jnp.int32, (tm, tm), 1)
    before = jnp.where(r_i < c_i, 1.0, 0.0).astype(_BF16)
    prefix = jnp.dot(member.astype(_BF16), before, preferred_element_type=_F32)
    base = carry_ref[...] + prefix
    ranks = [jnp.sum(jnp.where(h, base, 0.0), axis=0, keepdims=True) for h in hits]
    rank_ref[...] = jnp.concatenate(ranks, axis=0).astype(jnp.int32)
    total = carry_ref[...] + jnp.sum(member, axis=1, keepdims=True)
    carry_ref[...] = total
    cnt_ref[...] = total.astype(jnp.int32)


def _router(x, w_r, b_r, tm=512):
    t, d = x.shape
    kt = pl.BlockSpec((TOP_K, tm), lambda i: (0, i))
    return pl.pallas_call(
        _router_kernel,
        grid=(t // tm,),
        in_specs=[pl.BlockSpec((tm, d), lambda i: (i, 0)),
                  pl.BlockSpec((N_EXPERTS, d), lambda i: (0, 0)),
                  pl.BlockSpec((N_EXPERTS, 1), lambda i: (0, 0))],
        out_specs=[kt, kt, kt, pl.BlockSpec((N_EXPERTS, 1), lambda i: (0, 0))],
        out_shape=[jax.ShapeDtypeStruct((TOP_K, t), jnp.int32),
                   jax.ShapeDtypeStruct((TOP_K, t), _F32),
                   jax.ShapeDtypeStruct((TOP_K, t), jnp.int32),
                   jax.ShapeDtypeStruct((N_EXPERTS, 1), jnp.int32)],
        scratch_shapes=[pltpu.VMEM((N_EXPERTS, 1), _F32)],
        compiler_params=_params(1),
        name="router",
    )(x, w_r.T, b_r.astype(_F32).reshape(N_EXPERTS, 1))


def _experts_kernel(te_ref, nv_ref, nxt_ref, nn_ref, clo_ref, chi_ref, par_ref, tok_ref,
                    xs_hbm, wg_hbm, wu_hbm, wd_hbm, y_ref,
                    xbuf, wg_b, wu_b, wd_b, stg_a, stg_d, gsem, sem_a, sem_d, *, tm, layer):
    i = pl.program_id(0)
    n_valid = nv_ref[0]
    d = wg_hbm.shape[2]
    ka = d // _W_CHUNKS
    kd = D_EXPERT // _W_CHUNKS
    n_chunks = 3 * _W_CHUNKS

    def chunk_copy(c, e, w_hbm, rows, stg, sem):
        s2 = c % 2
        r0 = pl.multiple_of((c % _W_CHUNKS) * rows, rows)
        return pltpu.make_async_copy(w_hbm.at[layer, e, pl.ds(r0, rows), :], stg.at[s2], sem.at[s2])

    def start_chunk(c, e):
        @pl.when(c < _W_CHUNKS)
        def _():
            chunk_copy(c, e, wg_hbm, ka, stg_a, sem_a).start()

        @pl.when((c >= _W_CHUNKS) & (c < 2 * _W_CHUNKS))
        def _():
            chunk_copy(c, e, wu_hbm, ka, stg_a, sem_a).start()

        @pl.when(c >= 2 * _W_CHUNKS)
        def _():
            chunk_copy(c, e, wd_hbm, kd, stg_d, sem_d).start()

    def finish_chunk(c, e, slot):
        s2 = c % 2
        ra = pl.multiple_of((c % _W_CHUNKS) * ka, ka)
        rd = pl.multiple_of((c % _W_CHUNKS) * kd, kd)

        @pl.when(c < _W_CHUNKS)
        def _():
            chunk_copy(c, e, wg_hbm, ka, stg_a, sem_a).wait()
            wg_b[slot, pl.ds(ra, ka), :] = stg_a[s2].astype(_BF16)

        @pl.when((c >= _W_CHUNKS) & (c < 2 * _W_CHUNKS))
        def _():
            chunk_copy(c, e, wu_hbm, ka, stg_a, sem_a).wait()
            wu_b[slot, pl.ds(ra, ka), :] = stg_a[s2].astype(_BF16)

        @pl.when(c >= 2 * _W_CHUNKS)
        def _():
            chunk_copy(c, e, wd_hbm, kd, stg_d, sem_d).wait()
            wd_b[slot, pl.ds(rd, kd), :] = stg_d[s2].astype(_BF16)

    def load_chunks(lo, hi, e, slot):
        def body(c, carry):
            finish_chunk(c, e, slot)

            @pl.when(c + 2 < n_chunks)
            def _():
                start_chunk(c + 2, e)
            return carry
        lax.fori_loop(lo, hi, body, 0)

    def start_gather(tile, slot):
        base = tile * tm

        def body(r8, c):
            for u in range(_ISSUE_UNROLL):
                r = r8 * _ISSUE_UNROLL + u
                tok = tok_ref[base + r]
                pltpu.make_async_copy(xs_hbm.at[tok], xbuf.at[slot, r], gsem.at[slot]).start()
            return c
        lax.fori_loop(0, tm // _ISSUE_UNROLL, body, 0)

    @pl.when(i == 0)
    def _():
        start_gather(0, 0)
        start_chunk(0, te_ref[0])
        start_chunk(1, te_ref[0])
        load_chunks(0, n_chunks, te_ref[0], par_ref[0])

        @pl.when(nxt_ref[0] >= 0)
        def _():
            start_chunk(0, nxt_ref[0])
            start_chunk(1, nxt_ref[0])

    @pl.when(i + 1 < n_valid)
    def _():
        start_gather(i + 1, (i + 1) % 2)

    @pl.when(i < n_valid)
    def _():
        slot = par_ref[i]
        nxt = nxt_ref[i]
        lo = clo_ref[i]
        hi = chi_ref[i]
        third = (hi - lo + 2) // 3
        m1 = jnp.minimum(lo + third, hi)
        m2 = jnp.minimum(m1 + third, hi)

        def prefetch(c0, c1):
            @pl.when(nxt >= 0)
            def _():
                load_chunks(c0, c1, nxt, 1 - slot)

        prefetch(lo, m1)
        gslot = i % 2
        pltpu.make_async_copy(xs_hbm.at[pl.ds(0, tm)], xbuf.at[gslot], gsem.at[gslot]).wait()
        a = _from_slab(xbuf[gslot])
        g = jnp.dot(a, wg_b[slot], preferred_element_type=_F32)
        u = jnp.dot(a, wu_b[slot], preferred_element_type=_F32)
        prefetch(m1, m2)
        h = _silu_mul(g, u).astype(_BF16)
        y = jnp.dot(h, wd_b[slot], preferred_element_type=_F32)
        y_ref[...] = _to_slab(y.astype(_BF16))
        prefetch(m2, hi)

        @pl.when((nxt >= 0) & (hi == n_chunks) & (nn_ref[i] >= 0))
        def _():
            start_chunk(0, nn_ref[i])
            start_chunk(1, nn_ref[i])

    @pl.when(i >= n_valid)
    def _():
        y_ref[...] = jnp.zeros_like(y_ref)


def _experts(xs, meta, wg, wu, wd, layer, tm):
    t, sl, ln = xs.shape
    d = sl * ln
    n_tiles = meta[0].shape[0]
    ka = d // _W_CHUNKS
    kd = D_EXPERT // _W_CHUNKS
    any_spec = pl.BlockSpec(memory_space=pl.ANY)
    grid_spec = pltpu.PrefetchScalarGridSpec(
        num_scalar_prefetch=len(meta),
        grid=(n_tiles,),
        in_specs=[any_spec, any_spec, any_spec, any_spec],
        out_specs=pl.BlockSpec((tm, sl, ln), lambda i, *_: (i, 0, 0)),
        scratch_shapes=[pltpu.VMEM((2, tm, sl, ln), _BF16),
                        pltpu.VMEM((2, d, D_EXPERT), _BF16),
                        pltpu.VMEM((2, d, D_EXPERT), _BF16),
                        pltpu.VMEM((2, D_EXPERT, d), _BF16),
                        pltpu.VMEM((2, ka, D_EXPERT), _F32),
                        pltpu.VMEM((2, kd, d), _F32),
                        pltpu.SemaphoreType.DMA((2,)),
                        pltpu.SemaphoreType.DMA((2,)),
                        pltpu.SemaphoreType.DMA((2,))],
    )
    return pl.pallas_call(
        functools.partial(_experts_kernel, tm=tm, layer=layer),
        grid_spec=grid_spec,
        out_shape=jax.ShapeDtypeStruct((n_tiles * tm, sl, ln), _BF16),
        compiler_params=_params(1),
        name="experts",
    )(*meta, xs, wg, wu, wd)


def _ffn_kernel(a_ref, wg_ref, wu_ref, wd_ref, o_ref):
    a = a_ref[...]
    g = jnp.dot(a, wg_ref[...], preferred_element_type=_F32)
    u = jnp.dot(a, wu_ref[...], preferred_element_type=_F32)
    h = _silu_mul(g, u).astype(_BF16)
    o_ref[...] = jnp.dot(h, wd_ref[...], preferred_element_type=_F32)


def _shared_ffn(a, wg, wu, wd, tm=512):
    m, d = a.shape
    f = wg.shape[1]
    return pl.pallas_call(
        _ffn_kernel,
        grid=(m // tm,),
        in_specs=[pl.BlockSpec((tm, d), lambda i: (i, 0)),
                  pl.BlockSpec((d, f), lambda i: (0, 0)),
                  pl.BlockSpec((d, f), lambda i: (0, 0)),
                  pl.BlockSpec((f, d), lambda i: (0, 0))],
        out_specs=pl.BlockSpec((tm, d), lambda i: (i, 0)),
        out_shape=jax.ShapeDtypeStruct((m, d), _F32),
        compiler_params=_params(1),
        name="shared_ffn",
    )(a, wg, wu, wd)


def _combine_kernel(pos_ref, y_hbm, w3_ref, sh_ref, x_ref, g_ref, b_ref, o_ref, ob_ref, os_ref,
                    ybuf, sem, *, tc, n_tok):
    i = pl.program_id(0)
    n_steps = pl.num_programs(0)

    def start_gather(step, slot):
        for k in range(TOP_K):
            base = k * n_tok + step * tc

            def body(r8, c, k=k, base=base):
                for u in range(_ISSUE_UNROLL):
                    r = r8 * _ISSUE_UNROLL + u
                    p = pos_ref[base + r]
                    pltpu.make_async_copy(y_hbm.at[p], ybuf.at[slot, k, r], sem.at[slot]).start()
                return c
            lax.fori_loop(0, tc // _ISSUE_UNROLL, body, 0)

    @pl.when(i == 0)
    def _():
        start_gather(0, 0)

    @pl.when(i + 1 < n_steps)
    def _():
        start_gather(i + 1, (i + 1) % 2)

    slot = i % 2
    for k in range(TOP_K):
        pltpu.make_async_copy(y_hbm.at[pl.ds(0, tc)], ybuf.at[slot, k], sem.at[slot]).wait()
    acc = jnp.zeros(ybuf.shape[2:], _F32)
    for k in range(TOP_K):
        acc = acc + w3_ref[k] * ybuf[slot, k].astype(_F32)
    v = ALPHA * x_ref[...] + (_from_slab(acc) + sh_ref[...])
    y = _layer_norm_rows(v, g_ref[...], b_ref[...])
    yb = y.astype(_BF16)
    o_ref[...] = y
    ob_ref[...] = yb
    os_ref[...] = _to_slab(yb)


def _combine(pos_flat, ys, w3, shared, x, g, b, tc=64):
    t, d = x.shape
    sl = d // _LANES
    row = pl.BlockSpec((tc, d), lambda i, pos: (i, 0))
    slab = pl.BlockSpec((tc, sl, _LANES), lambda i, pos: (i, 0, 0))
    vec = pl.BlockSpec((1, d), lambda i, pos: (0, 0))
    grid_spec = pltpu.PrefetchScalarGridSpec(
        num_scalar_prefetch=1,
        grid=(t // tc,),
        in_specs=[pl.BlockSpec(memory_space=pl.ANY),
                  pl.BlockSpec((TOP_K, tc, 1, _LANES), lambda i, pos: (0, i, 0, 0)),
                  row, row, vec, vec],
        out_specs=[row, row, slab],
        scratch_shapes=[pltpu.VMEM((2, TOP_K, tc, sl, _LANES), _BF16),
                        pltpu.SemaphoreType.DMA((2,))],
    )
    return pl.pallas_call(
        functools.partial(_combine_kernel, tc=tc, n_tok=t),
        grid_spec=grid_spec,
        out_shape=[jax.ShapeDtypeStruct((t, d), _F32), jax.ShapeDtypeStruct((t, d), _BF16),
                   jax.ShapeDtypeStruct((t, sl, _LANES), _BF16)],
        compiler_params=_params(1),
        name="combine",
    )(pos_flat, ys, w3, shared, x, g.reshape(1, d), b.reshape(1, d))


def _moe_block(x, xb, xs, layer, w_r, b_r, wg, wu, wd, sg, su, sd, g, b, tm=256):
    t, d = x.shape
    idx_t, w_t, rank_t, cnt = _router(x, w_r, b_r)

    counts = cnt[:, 0]
    tiles_e = (counts + tm - 1) // tm
    tile_end = jnp.cumsum(tiles_e)
    tile_start = tile_end - tiles_e
    row_off = tile_start * tm
    n_valid = tile_end[-1:]
    n_tiles = t * TOP_K // tm + N_EXPERTS
    e_ids = jnp.arange(N_EXPERTS, dtype=jnp.int32)
    pos = jnp.sum(jnp.where(idx_t[:, :, None] == e_ids, row_off, 0), axis=-1) + rank_t
    tok = jnp.broadcast_to(jnp.arange(t, dtype=jnp.int32), (TOP_K, t))
    row_token = jnp.zeros((n_tiles * tm,), jnp.int32).at[pos.reshape(-1)].set(
        tok.reshape(-1), unique_indices=True, indices_are_sorted=False)
    tile_ids = jnp.minimum(jnp.arange(n_tiles, dtype=jnp.int32), n_valid - 1)
    onehot_te = (tile_end[None, :] <= tile_ids[:, None]).astype(jnp.int32)
    tile_expert = jnp.sum(onehot_te, axis=1)

    has = tiles_e > 0
    later = (e_ids[None, :] > e_ids[:, None]) & has[None, :]
    nxt_e = jnp.min(jnp.where(later, e_ids[None, :], N_EXPERTS), axis=1)
    nxt_e = jnp.where(nxt_e == N_EXPERTS, -1, nxt_e)
    nxt_onehot = nxt_e[:, None] == e_ids[None, :]
    nn_e = jnp.where(nxt_e >= 0, jnp.sum(jnp.where(nxt_onehot, nxt_e[None, :], 0), axis=1), -1)
    par_e = (jnp.cumsum(has.astype(jnp.int32)) - has.astype(jnp.int32)) % 2

    def per_tile(v):
        return jnp.sum(jnp.where(tile_expert[:, None] == e_ids[None, :], v[None, :], 0), axis=1)
    n_run = jnp.maximum(per_tile(tiles_e), 1)
    p_run = tile_ids - per_tile(tile_start)
    n_chunks = 3 * _W_CHUNKS
    meta = (tile_expert, n_valid.astype(jnp.int32), per_tile(nxt_e), per_tile(nn_e),
            (n_chunks * p_run) // n_run, (n_chunks * (p_run + 1)) // n_run, per_tile(par_e),
            row_token)
    meta = tuple(m.astype(jnp.int32) for m in meta)

    ys = _experts(xs, meta, wg, wu, wd, layer, tm)
    shared = _shared_ffn(xb, sg.astype(_BF16), su.astype(_BF16), sd.astype(_BF16))
    w3 = jnp.broadcast_to(w_t[:, :, None, None], (TOP_K, t, 1, _LANES))
    return _combine(pos.reshape(-1), ys, w3, shared, x, g, b)


def _fgate_kernel(x_ref, wt_ref, b_ref, c_ref, carry_ref):
    s = pl.program_id(1)

    @pl.when(s == 0)
    def _():
        carry_ref[...] = jnp.zeros_like(carry_ref)

    ts = x_ref.shape[0]
    z = lax.dot_general(wt_ref[...], x_ref[...], (((1,), (1,)), ((), ())),
                        precision=lax.Precision.HIGHEST,
                        preferred_element_type=_F32) + b_ref[...]
    ls = jnp.minimum(z, 0.0) - jnp.log1p(jnp.exp(-jnp.abs(z)))
    r_i = lax.broadcasted_iota(jnp.int32, (ts, ts), 0)
    c_i = lax.broadcasted_iota(jnp.int32, (ts, ts), 1)
    upto = jnp.where(r_i <= c_i, 1.0, 0.0)
    c = carry_ref[...] + jnp.dot(ls, upto, precision=lax.Precision.HIGHEST,
                                 preferred_element_type=_F32)
    c_ref[...] = c
    carry_ref[...] = c[:, ts - 1:ts]


def _fgate(x3, w_f, b_f, ts=512):
    bsz, s, d = x3.shape
    return pl.pallas_call(
        _fgate_kernel,
        grid=(bsz, s // ts),
        in_specs=[pl.BlockSpec((None, ts, d), lambda bb, i: (bb, i, 0)),
                  pl.BlockSpec((N_HEADS, d), lambda bb, i: (0, 0)),
                  pl.BlockSpec((N_HEADS, 1), lambda bb, i: (0, 0))],
        out_specs=pl.BlockSpec((None, N_HEADS, ts), lambda bb, i: (bb, 0, i)),
        out_shape=jax.ShapeDtypeStruct((bsz, N_HEADS, s), _F32),
        scratch_shapes=[pltpu.VMEM((N_HEADS, 1), _F32)],
        compiler_params=_params(2),
        name="fgate",
    )(x3, w_f.T, b_f.astype(_F32).reshape(N_HEADS, 1))


def _attn_kernel(q_ref, k_ref, v_ref, cq_ref, ck_ref, o_ref, *, tq, G):
    hp = pl.program_id(1)
    qi = pl.program_id(2)
    qk_scale = HEAD_DIM ** -0.5 * _LOG2E
    lane = lax.broadcasted_iota(jnp.int32, cq_ref.shape, 1)
    cqs = [_LOG2E * jnp.sum(jnp.where(lane == hp * G + g, cq_ref[...], 0.0), axis=1, keepdims=True)
           for g in range(G)]
    qs = [q_ref[:, g * HEAD_DIM:(g + 1) * HEAD_DIM] for g in range(G)]

    def step(j, carry, diagonal):
        k0 = pl.multiple_of(j * tq, tq)
        out = []
        for g in range(G):
            m, l, acc = carry[g]
            kj = k_ref[pl.ds(k0, tq), g * HEAD_DIM:(g + 1) * HEAD_DIM]
            vj = v_ref[pl.ds(k0, tq), g * HEAD_DIM:(g + 1) * HEAD_DIM]
            ck = _LOG2E * ck_ref[g:g + 1, pl.ds(k0, tq)]
            s = lax.dot_general(qs[g], kj, (((1,), (1,)), ((), ())), preferred_element_type=_F32)
            s = s * qk_scale - ck
            if diagonal:
                r_i = lax.broadcasted_iota(jnp.int32, s.shape, 0)
                c_i = lax.broadcasted_iota(jnp.int32, s.shape, 1)
                s = jnp.where(c_i <= r_i, s, _NEG)
            m_new = jnp.maximum(m, jnp.max(s, axis=1, keepdims=True) + cqs[g])
            alpha = jnp.exp2(m - m_new)
            p = jnp.exp2(s - (m_new - cqs[g]))
            l = alpha * l + jnp.sum(p, axis=1, keepdims=True)
            acc = alpha * acc + jnp.dot(p.astype(_BF16), vj, preferred_element_type=_F32)
            out.append((m_new, l, acc))
        return tuple(out)

    init = tuple((jnp.full((tq, 1), _NEG, _F32), jnp.zeros((tq, 1), _F32),
                  jnp.zeros((tq, HEAD_DIM), _F32)) for _ in range(G))
    carry = lax.fori_loop(0, qi, lambda j, c: step(j, c, False), init)
    fin = step(qi, carry, True)
    for g in range(G):
        _, l, acc = fin[g]
        o_ref[:, g * HEAD_DIM:(g + 1) * HEAD_DIM] = (acc / l).astype(o_ref.dtype)


def _attention(q3, kv3, c_sh, c_hs, tq=1024, G=_HEADS_PER_STEP):
    bsz, s, d = q3.shape
    w = G * HEAD_DIM
    nh = N_HEADS // G
    return pl.pallas_call(
        functools.partial(_attn_kernel, tq=tq, G=G),
        grid=(bsz, nh, s // tq),
        in_specs=[pl.BlockSpec((None, tq, w), lambda bb, h, i: (bb, i, h)),
                  pl.BlockSpec((None, s, w), lambda bb, h, i: (bb, 0, h)),
                  pl.BlockSpec((None, s, w), lambda bb, h, i: (bb, 0, nh + h)),
                  pl.BlockSpec((None, tq, N_HEADS), lambda bb, h, i: (bb, i, 0)),
                  pl.BlockSpec((None, None, G, s), lambda bb, h, i: (bb, h, 0, 0))],
        out_specs=pl.BlockSpec((None, tq, w), lambda bb, h, i: (bb, i, h)),
        out_shape=jax.ShapeDtypeStruct((bsz, s, d), _BF16),
        compiler_params=_params(3),
        name="fox_attention",
    )(q3, kv3, kv3, c_sh, c_hs.reshape(bsz, nh, G, s))


def kernel(x, ln_g, ln_b, conv_w_in, conv_w, conv_w_out, kv_w, kv_fb, attn_w_q, attn_w_o,
           router_w, router_b, exp_w_gate, exp_w_up, exp_w_down,
           shared_w_gate, shared_w_up, shared_w_down):
    bsz, seq, d = x.shape
    t = bsz * seq
    xf = x.reshape(t, d)
    xb = xf.astype(_BF16)
    kv3 = c_sh = c_hs = None
    for l in range(DEPTH):
        if l < N_A_LAYERS:
            y = _conv_in(xb, conv_w_in, conv_w, l, seq)
            h = _mm(y, conv_w_out, l, d, _F32)
        else:
            j = l - N_A_LAYERS
            if l == N_A_LAYERS:
                kv = _mm(xb, kv_w.reshape(1, d, kv_w.shape[1]), 0, 2 * d, _BF16)
                kv3 = kv.reshape(bsz, seq, 2 * d)
                c_hs = _fgate(xf.reshape(bsz, seq, d), kv_w[:, 2 * d:], kv_fb)
                c_sh = jnp.transpose(c_hs, (0, 2, 1))
            q = _mm(xb, attn_w_q, j, d, _BF16)
            o = _attention(q.reshape(bsz, seq, d), kv3, c_sh, c_hs)
            h = _mm(o.reshape(t, d), attn_w_o, j, d, _F32)
        xf, xb, xs = _res_ln(xf, h, ln_g[l, 0], ln_b[l, 0])
        xf, xb, xs = _moe_block(xf, xb, xs, l, router_w[l], router_b[l],
                                exp_w_gate, exp_w_up, exp_w_down,
                                shared_w_gate[l], shared_w_up[l], shared_w_down[l],
                                ln_g[l, 1], ln_b[l, 1])
    return xf.reshape(bsz, seq, d)
```

```python
import functools
import math

import jax
import jax.numpy as jnp
from jax import lax
from jax.experimental import pallas as pl
from jax.experimental.pallas import tpu as pltpu

D_MODEL = 4096
DEPTH = 2
N_A_LAYERS = DEPTH // 2
HEAD_DIM = 128
N_HEADS = D_MODEL // HEAD_DIM
CONV_WIDTH = 3
N_EXPERTS = 64
N_GROUPS = 8
GROUP_SIZE = N_EXPERTS // N_GROUPS
TOPK_GROUPS = 4
TOP_K = 8
D_EXPERT = 384
ROUTED_SCALE = 2.5
ALPHA = (2.0 * DEPTH) ** 0.25
LN_EPS = 1e-5

_MIB = 1024 * 1024
_VMEM_LIMIT = 56 * _MIB
_NEG = -1e30
_LOG2E = math.log2(math.e)
_ISSUE_UNROLL = 8
_W_CHUNKS = 4
_LANES = 128
_HEADS_PER_STEP = 2

_F32 = jnp.float32
_BF16 = jnp.bfloat16


def _params(n_axes, vmem=_VMEM_LIMIT):
    return pltpu.CompilerParams(dimension_semantics=("arbitrary",) * n_axes,
                                vmem_limit_bytes=vmem)


def _layer_norm_rows(v, g, b):
    mu = jnp.mean(v, axis=-1, keepdims=True)
    xc = v - mu
    var = jnp.mean(xc * xc, axis=-1, keepdims=True)
    return xc * lax.rsqrt(var + LN_EPS) * g + b


def _silu_mul(g, u):
    return g * jax.nn.sigmoid(g) * u


def _to_slab(v):
    return v.reshape(v.shape[0], v.shape[1] // _LANES, _LANES)


def _from_slab(v):
    return v.reshape(v.shape[0], v.shape[1] * v.shape[2])


def _mm_kernel(a_ref, w_ref, o_ref, wb_ref):
    @pl.when(pl.program_id(1) == 0)
    def _():
        wb_ref[...] = w_ref[...].astype(_BF16)

    o_ref[...] = jnp.dot(a_ref[...], wb_ref[...],
                         preferred_element_type=_F32).astype(o_ref.dtype)


def _mm(a, w, layer, n, out_dtype, tm=1024, tn=512):
    m, k = a.shape
    if layer is None:
        w_spec = pl.BlockSpec((k, tn), lambda j, i: (0, j))
    else:
        w_spec = pl.BlockSpec((None, k, tn), lambda j, i: (layer, 0, j))
    return pl.pallas_call(
        _mm_kernel,
        grid=(n // tn, m // tm),
        in_specs=[pl.BlockSpec((tm, k), lambda j, i: (i, 0)), w_spec],
        out_specs=pl.BlockSpec((tm, tn), lambda j, i: (i, j)),
        out_shape=jax.ShapeDtypeStruct((m, n), out_dtype),
        scratch_shapes=[pltpu.VMEM((k, tn), _BF16)],
        compiler_params=_params(2),
        name="dense_mm",
    )(a, w)


def _conv_in_kernel(a_ref, wb_ref, wc_ref, wu_ref, cw_ref, y_ref, wbf_ref, carry_ref,
                    *, tiles_per_seq):
    i = pl.program_id(1)

    @pl.when(i == 0)
    def _():
        wbf_ref[0] = wb_ref[...].astype(_BF16)
        wbf_ref[1] = wc_ref[...].astype(_BF16)
        wbf_ref[2] = wu_ref[...].astype(_BF16)

    @pl.when(i % tiles_per_seq == 0)
    def _():
        carry_ref[...] = jnp.zeros_like(carry_ref)

    a = a_ref[...]
    gate_out = jnp.dot(a, wbf_ref[0], preferred_element_type=_F32)
    gate_in = jnp.dot(a, wbf_ref[1], preferred_element_type=_F32)
    u = jnp.dot(a, wbf_ref[2], preferred_element_type=_F32)
    z = gate_in * u
    tm = z.shape[0]
    row = lax.broadcasted_iota(jnp.int32, z.shape, 0)
    p1 = carry_ref[7:8, :]
    p2 = carry_ref[6:7, :]
    z1 = jnp.where(row == 0, p1, pltpu.roll(z, 1, 0))
    z2 = jnp.where(row == 0, p2, jnp.where(row == 1, p1, pltpu.roll(z, 2, 0)))
    cw = cw_ref[...]
    conv = cw[0:1, :] * z2 + cw[1:2, :] * z1 + cw[2:3, :] * z
    y_ref[...] = (gate_out * conv).astype(y_ref.dtype)
    carry_ref[...] = z[tm - 8:tm, :]


def _conv_in(a, w_in, conv_w, layer, seq, tm=1024, tn=256):
    m, k = a.shape
    d = w_in.shape[2] // 3
    nb = d // tn
    kern = functools.partial(_conv_in_kernel, tiles_per_seq=seq // tm)
    return pl.pallas_call(
        kern,
        grid=(nb, m // tm),
        in_specs=[pl.BlockSpec((tm, k), lambda j, i: (i, 0)),
                  pl.BlockSpec((None, k, tn), lambda j, i: (layer, 0, j)),
                  pl.BlockSpec((None, k, tn), lambda j, i: (layer, 0, j + nb)),
                  pl.BlockSpec((None, k, tn), lambda j, i: (layer, 0, j + 2 * nb)),
                  pl.BlockSpec((None, CONV_WIDTH, tn), lambda j, i: (layer, 0, j))],
        out_specs=pl.BlockSpec((tm, tn), lambda j, i: (i, j)),
        out_shape=jax.ShapeDtypeStruct((m, d), _BF16),
        scratch_shapes=[pltpu.VMEM((3, k, tn), _BF16), pltpu.VMEM((8, tn), _F32)],
        compiler_params=_params(2),
        name="conv_in",
    )(a, w_in, w_in, w_in, conv_w)


def _res_ln_kernel(x_ref, h_ref, g_ref, b_ref, o_ref, ob_ref, os_ref):
    v = ALPHA * x_ref[...] + h_ref[...]
    y = _layer_norm_rows(v, g_ref[...], b_ref[...])
    yb = y.astype(_BF16)
    o_ref[...] = y
    ob_ref[...] = yb
    os_ref[...] = _to_slab(yb)


def _res_ln(x, h, g, b, tm=256):
    m, d = x.shape
    row = pl.BlockSpec((tm, d), lambda i: (i, 0))
    vec = pl.BlockSpec((1, d), lambda i: (0, 0))
    return pl.pallas_call(
        _res_ln_kernel,
        grid=(m // tm,),
        in_specs=[row, row, vec, vec],
        out_specs=[row, row, pl.BlockSpec((tm, d // _LANES, _LANES), lambda i: (i, 0, 0))],
        out_shape=[jax.ShapeDtypeStruct((m, d), _F32), jax.ShapeDtypeStruct((m, d), _BF16),
                   jax.ShapeDtypeStruct((m, d // _LANES, _LANES), _BF16)],
        compiler_params=_params(1),
        name="res_ln",
    )(x, h, g.reshape(1, d), b.reshape(1, d))


def _first_index_of_max(vals, axis, size):
    mx = jnp.max(vals, axis=axis, keepdims=True)
    idx = lax.broadcasted_iota(jnp.int32, vals.shape, axis)
    first = jnp.min(jnp.where(vals == mx, idx, size), axis=axis, keepdims=True)
    return mx, idx == first


def _router_kernel(x_ref, wt_ref, b_ref, idx_ref, w_ref, rank_ref, cnt_ref, carry_ref):
    i = pl.program_id(0)

    @pl.when(i == 0)
    def _():
        carry_ref[...] = jnp.zeros_like(carry_ref)

    tm = x_ref.shape[0]
    logits = lax.dot_general(wt_ref[...], x_ref[...], (((1,), (1,)), ((), ())),
                             precision=lax.Precision.HIGHEST,
                             preferred_element_type=_F32)
    scores = jax.nn.sigmoid(logits)
    choice = scores + b_ref[...]
    c3 = choice.reshape(N_GROUPS, GROUP_SIZE, tm)
    m1, hit1 = _first_index_of_max(c3, 1, GROUP_SIZE)
    m2 = jnp.max(jnp.where(hit1, _NEG, c3), axis=1, keepdims=True)
    cur = (m1 + m2).reshape(N_GROUPS, tm)
    sel = jnp.zeros((N_GROUPS, tm), _F32)
    for _ in range(TOPK_GROUPS):
        _, hit = _first_index_of_max(cur, 0, N_GROUPS)
        sel = jnp.where(hit, 1.0, sel)
        cur = jnp.where(hit, _NEG, cur)
    exp_mask = jnp.broadcast_to(sel.reshape(N_GROUPS, 1, tm),
                                (N_GROUPS, GROUP_SIZE, tm)).reshape(N_EXPERTS, tm)
    masked = jnp.where(exp_mask > 0.0, choice, _NEG)

    eidx = lax.broadcasted_iota(jnp.int32, (N_EXPERTS, tm), 0)
    hits, idxs, ws = [], [], []
    member = jnp.zeros((N_EXPERTS, tm), _F32)
    for _ in range(TOP_K):
        _, hit = _first_index_of_max(masked, 0, N_EXPERTS)
        hits.append(hit)
        idxs.append(jnp.sum(jnp.where(hit, eidx, 0), axis=0, keepdims=True))
        ws.append(jnp.sum(jnp.where(hit, scores, 0.0), axis=0, keepdims=True))
        member = jnp.where(hit, 1.0, member)
        masked = jnp.where(hit, _NEG, masked)
    w = jnp.concatenate(ws, axis=0)
    w = w / jnp.sum(w, axis=0, keepdims=True) * ROUTED_SCALE
    idx_ref[...] = jnp.concatenate(idxs, axis=0)
    w_ref[...] = w

    r_i = lax.broadcasted_iota(jnp.int32, (tm, tm), 0)
    c_i = lax.broadcasted_iota(jnp.int32, (tm, tm), 1)
    before = jnp.where(r_i < c_i, 1.0, 0.0).astype(_BF16)
    prefix = jnp.dot(member.astype(_BF16), before, preferred_element_type=_F32)
    base = carry_ref[...] + prefix
    ranks = [jnp.sum(jnp.where(h, base, 0.0), axis=0, keepdims=True) for h in hits]
    rank_ref[...] = jnp.concatenate(ranks, axis=0).astype(jnp.int32)
    total = carry_ref[...] + jnp.sum(member, axis=1, keepdims=True)
    carry_ref[...] = total
    cnt_ref[...] = total.astype(jnp.int32)


def _router(x, w_r, b_r, tm=512):
    t, d = x.shape
    kt = pl.BlockSpec((TOP_K, tm), lambda i: (0, i))
    return pl.pallas_call(
        _router_kernel,
        grid=(t // tm,),
        in_specs=[pl.BlockSpec((tm, d), lambda i: (i, 0)),
                  pl.BlockSpec((N_EXPERTS, d), lambda i: (0, 0)),
                  pl.BlockSpec((N_EXPERTS, 1), lambda i: (0, 0))],
        out_specs=[kt, kt, kt, pl.BlockSpec((N_EXPERTS, 1), lambda i: (0, 0))],
        out_shape=[jax.ShapeDtypeStruct((TOP_K, t), jnp.int32),
                   jax.ShapeDtypeStruct((TOP_K, t), _F32),
                   jax.ShapeDtypeStruct((TOP_K, t), jnp.int32),
                   jax.ShapeDtypeStruct((N_EXPERTS, 1), jnp.int32)],
        scratch_shapes=[pltpu.VMEM((N_EXPERTS, 1), _F32)],
        compiler_params=_params(1),
        name="router",
    )(x, w_r.T, b_r.astype(_F32).reshape(N_EXPERTS, 1))


def _experts_kernel(te_ref, nv_ref, nxt_ref, nn_ref, clo_ref, chi_ref, par_ref, tok_ref,
                    xs_hbm, wg_hbm, wu_hbm, wd_hbm, y_ref,
                    xbuf, wg_b, wu_b, wd_b, stg_a, stg_d, gsem, sem_a, sem_d, *, tm, layer):
    i = pl.program_id(0)
    n_valid = nv_ref[0]
    d = wg_hbm.shape[2]
    ka = d // _W_CHUNKS
    kd = D_EXPERT // _W_CHUNKS
    n_chunks = 3 * _W_CHUNKS

    def chunk_copy(c, e, w_hbm, rows, stg, sem):
        s2 = c % 2
        r0 = pl.multiple_of((c % _W_CHUNKS) * rows, rows)
        return pltpu.make_async_copy(w_hbm.at[layer, e, pl.ds(r0, rows), :], stg.at[s2], sem.at[s2])

    def start_chunk(c, e):
        @pl.when(c < _W_CHUNKS)
        def _():
            chunk_copy(c, e, wg_hbm, ka, stg_a, sem_a).start()

        @pl.when((c >= _W_CHUNKS) & (c < 2 * _W_CHUNKS))
        def _():
            chunk_copy(c, e, wu_hbm, ka, stg_a, sem_a).start()

        @pl.when(c >= 2 * _W_CHUNKS)
        def _():
            chunk_copy(c, e, wd_hbm, kd, stg_d, sem_d).start()

    def finish_chunk(c, e, slot):
        s2 = c % 2
        ra = pl.multiple_of((c % _W_CHUNKS) * ka, ka)
        rd = pl.multiple_of((c % _W_CHUNKS) * kd, kd)

        @pl.when(c < _W_CHUNKS)
        def _():
            chunk_copy(c, e, wg_hbm, ka, stg_a, sem_a).wait()
            wg_b[slot, pl.ds(ra, ka), :] = stg_a[s2].astype(_BF16)

        @pl.when((c >= _W_CHUNKS) & (c < 2 * _W_CHUNKS))
        def _():
            chunk_copy(c, e, wu_hbm, ka, stg_a, sem_a).wait()
            wu_b[slot, pl.ds(ra, ka), :] = stg_a[s2].astype(_BF16)

        @pl.when(c >= 2 * _W_CHUNKS)
        def _():
            chunk_copy(c, e, wd_hbm, kd, stg_d, sem_d).wait()
            wd_b[slot, pl.ds(rd, kd), :] = stg_d[s2].astype(_BF16)

    def load_chunks(lo, hi, e, slot):
        def body(c, carry):
            finish_chunk(c, e, slot)

            @pl.when(c + 2 < n_chunks)
            def _():
                start_chunk(c + 2, e)
            return carry
        lax.fori_loop(lo, hi, body, 0)

    def start_gather(tile, slot):
        base = tile * tm

        def body(r8, c):
            for u in range(_ISSUE_UNROLL):
                r = r8 * _ISSUE_UNROLL + u
                tok = tok_ref[base + r]
                pltpu.make_async_copy(xs_hbm.at[tok], xbuf.at[slot, r], gsem.at[slot]).start()
            return c
        lax.fori_loop(0, tm // _ISSUE_UNROLL, body, 0)

    @pl.when(i == 0)
    def _():
        start_gather(0, 0)
        start_chunk(0, te_ref[0])
        start_chunk(1, te_ref[0])
        load_chunks(0, n_chunks, te_ref[0], par_ref[0])

        @pl.when(nxt_ref[0] >= 0)
        def _():
            start_chunk(0, nxt_ref[0])
            start_chunk(1, nxt_ref[0])

    @pl.when(i + 1 < n_valid)
    def _():
        start_gather(i + 1, (i + 1) % 2)

    @pl.when(i < n_valid)
    def _():
        slot = par_ref[i]
        nxt = nxt_ref[i]
        lo = clo_ref[i]
        hi = chi_ref[i]
        third = (hi - lo + 2) // 3
        m1 = jnp.minimum(lo + third, hi)
        m2 = jnp.minimum(m1 + third, hi)

        def prefetch(c0, c1):
            @pl.when(nxt >= 0)
            def _():
                load_chunks(c0, c1, nxt, 1 - slot)

        prefetch(lo, m1)
        gslot = i % 2
        pltpu.make_async_copy(xs_hbm.at[pl.ds(0, tm)], xbuf.at[gslot], gsem.at[gslot]).wait()
        a = _from_slab(xbuf[gslot])
        g = jnp.dot(a, wg_b[slot], preferred_element_type=_F32)
        u = jnp.dot(a, wu_b[slot], preferred_element_type=_F32)
        prefetch(m1, m2)
        h = _silu_mul(g, u).astype(_BF16)
        y = jnp.dot(h, wd_b[slot], preferred_element_type=_F32)
        y_ref[...] = _to_slab(y.astype(_BF16))
        prefetch(m2, hi)

        @pl.when((nxt >= 0) & (hi == n_chunks) & (nn_ref[i] >= 0))
        def _():
            start_chunk(0, nn_ref[i])
            start_chunk(1, nn_ref[i])

    @pl.when(i >= n_valid)
    def _():
        y_ref[...] = jnp.zeros_like(y_ref)


def _experts(xs, meta, wg, wu, wd, layer, tm):
    t, sl, ln = xs.shape
    d = sl * ln
    n_tiles = meta[0].shape[0]
    ka = d // _W_CHUNKS
    kd = D_EXPERT // _W_CHUNKS
    any_spec = pl.BlockSpec(memory_space=pl.ANY)
    grid_spec = pltpu.PrefetchScalarGridSpec(
        num_scalar_prefetch=len(meta),
        grid=(n_tiles,),
        in_specs=[any_spec, any_spec, any_spec, any_spec],
        out_specs=pl.BlockSpec((tm, sl, ln), lambda i, *_: (i, 0, 0)),
        scratch_shapes=[pltpu.VMEM((2, tm, sl, ln), _BF16),
                        pltpu.VMEM((2, d, D_EXPERT), _BF16),
                        pltpu.VMEM((2, d, D_EXPERT), _BF16),
                        pltpu.VMEM((2, D_EXPERT, d), _BF16),
                        pltpu.VMEM((2, ka, D_EXPERT), _F32),
                        pltpu.VMEM((2, kd, d), _F32),
                        pltpu.SemaphoreType.DMA((2,)),
                        pltpu.SemaphoreType.DMA((2,)),
                        pltpu.SemaphoreType.DMA((2,))],
    )
    return pl.pallas_call(
        functools.partial(_experts_kernel, tm=tm, layer=layer),
        grid_spec=grid_spec,
        out_shape=jax.ShapeDtypeStruct((n_tiles * tm, sl, ln), _BF16),
        compiler_params=_params(1),
        name="experts",
    )(*meta, xs, wg, wu, wd)


def _ffn_kernel(a_ref, wg_ref, wu_ref, wd_ref, o_ref):
    a = a_ref[...]
    g = jnp.dot(a, wg_ref[...], preferred_element_type=_F32)
    u = jnp.dot(a, wu_ref[...], preferred_element_type=_F32)
    h = _silu_mul(g, u).astype(_BF16)
    o_ref[...] = jnp.dot(h, wd_ref[...], preferred_element_type=_F32)


def _shared_ffn(a, wg, wu, wd, tm=512):
    m, d = a.shape
    f = wg.shape[1]
    return pl.pallas_call(
        _ffn_kernel,
        grid=(m // tm,),
        in_specs=[pl.BlockSpec((tm, d), lambda i: (i, 0)),
                  pl.BlockSpec((d, f), lambda i: (0, 0)),
                  pl.BlockSpec((d, f), lambda i: (0, 0)),
                  pl.BlockSpec((f, d), lambda i: (0, 0))],
        out_specs=pl.BlockSpec((tm, d), lambda i: (i, 0)),
        out_shape=jax.ShapeDtypeStruct((m, d), _F32),
        compiler_params=_params(1),
        name="shared_ffn",
    )(a, wg, wu, wd)


def _combine_kernel(pos_ref, y_hbm, w_ref, sh_ref, x_ref, g_ref, b_ref, o_ref, ob_ref, os_ref,
                    ybuf, sem, *, tc, n_tok):
    i = pl.program_id(0)
    n_steps = pl.num_programs(0)

    def start_gather(step, slot):
        for k in range(TOP_K):
            base = k * n_tok + step * tc

            def body(r8, c, k=k, base=base):
                for u in range(_ISSUE_UNROLL):
                    r = r8 * _ISSUE_UNROLL + u
                    p = pos_ref[base + r]
                    pltpu.make_async_copy(y_hbm.at[p], ybuf.at[slot, k, r], sem.at[slot]).start()
                return c
            lax.fori_loop(0, tc // _ISSUE_UNROLL, body, 0)

    @pl.when(i == 0)
    def _():
        start_gather(0, 0)

    @pl.when(i + 1 < n_steps)
    def _():
        start_gather(i + 1, (i + 1) % 2)

    slot = i % 2
    for k in range(TOP_K):
        pltpu.make_async_copy(y_hbm.at[pl.ds(0, tc)], ybuf.at[slot, k], sem.at[slot]).wait()
    w = w_ref[...]
    acc = jnp.zeros(ybuf.shape[2:], _F32)
    for k in range(TOP_K):
        acc = acc + w[:, k:k + 1].reshape(tc, 1, 1) * ybuf[slot, k].astype(_F32)
    v = ALPHA * x_ref[...] + (_from_slab(acc) + sh_ref[...])
    y = _layer_norm_rows(v, g_ref[...], b_ref[...])
    yb = y.astype(_BF16)
    o_ref[...] = y
    ob_ref[...] = yb
    os_ref[...] = _to_slab(yb)


def _combine(pos_flat, ys, w_tok, shared, x, g, b, tc=64):
    t, d = x.shape
    sl = d // _LANES
    row = pl.BlockSpec((tc, d), lambda i, pos: (i, 0))
    slab = pl.BlockSpec((tc, sl, _LANES), lambda i, pos: (i, 0, 0))
    vec = pl.BlockSpec((1, d), lambda i, pos: (0, 0))
    grid_spec = pltpu.PrefetchScalarGridSpec(
        num_scalar_prefetch=1,
        grid=(t // tc,),
        in_specs=[pl.BlockSpec(memory_space=pl.ANY),
                  pl.BlockSpec((tc, TOP_K), lambda i, pos: (i, 0)),
                  row, row, vec, vec],
        out_specs=[row, row, slab],
        scratch_shapes=[pltpu.VMEM((2, TOP_K, tc, sl, _LANES), _BF16),
                        pltpu.SemaphoreType.DMA((2,))],
    )
    return pl.pallas_call(
        functools.partial(_combine_kernel, tc=tc, n_tok=t),
        grid_spec=grid_spec,
        out_shape=[jax.ShapeDtypeStruct((t, d), _F32), jax.ShapeDtypeStruct((t, d), _BF16),
                   jax.ShapeDtypeStruct((t, sl, _LANES), _BF16)],
        compiler_params=_params(1),
        name="combine",
    )(pos_flat, ys, w_tok, shared, x, g.reshape(1, d), b.reshape(1, d))


def _moe_block(x, xb, xs, layer, w_r, b_r, wg, wu, wd, sg, su, sd, g, b, tm=256):
    t, d = x.shape
    idx_t, w_t, rank_t, cnt = _router(x, w_r, b_r)

    counts = cnt[:, 0]
    tiles_e = (counts + tm - 1) // tm
    tile_end = jnp.cumsum(tiles_e)
    tile_start = tile_end - tiles_e
    row_off = tile_start * tm
    n_valid = tile_end[-1:]
    n_tiles = t * TOP_K // tm + N_EXPERTS
    e_ids = jnp.arange(N_EXPERTS, dtype=jnp.int32)
    pos = jnp.sum(jnp.where(idx_t[:, :, None] == e_ids, row_off, 0), axis=-1) + rank_t
    tok = jnp.broadcast_to(jnp.arange(t, dtype=jnp.int32), (TOP_K, t))
    row_token = jnp.zeros((n_tiles * tm,), jnp.int32).at[pos.reshape(-1)].set(
        tok.reshape(-1), unique_indices=True, indices_are_sorted=False)
    tile_ids = jnp.minimum(jnp.arange(n_tiles, dtype=jnp.int32), n_valid - 1)
    onehot_te = (tile_end[None, :] <= tile_ids[:, None]).astype(jnp.int32)
    tile_expert = jnp.sum(onehot_te, axis=1)

    has = tiles_e > 0
    later = (e_ids[None, :] > e_ids[:, None]) & has[None, :]
    nxt_e = jnp.min(jnp.where(later, e_ids[None, :], N_EXPERTS), axis=1)
    nxt_e = jnp.where(nxt_e == N_EXPERTS, -1, nxt_e)
    nxt_onehot = nxt_e[:, None] == e_ids[None, :]
    nn_e = jnp.where(nxt_e >= 0, jnp.sum(jnp.where(nxt_onehot, nxt_e[None, :], 0), axis=1), -1)
    par_e = (jnp.cumsum(has.astype(jnp.int32)) - has.astype(jnp.int32)) % 2

    def per_tile(v):
        return jnp.sum(jnp.where(tile_expert[:, None] == e_ids[None, :], v[None, :], 0), axis=1)
    n_run = jnp.maximum(per_tile(tiles_e), 1)
    p_run = tile_ids - per_tile(tile_start)
    n_chunks = 3 * _W_CHUNKS
    meta = (tile_expert, n_valid.astype(jnp.int32), per_tile(nxt_e), per_tile(nn_e),
            (n_chunks * p_run) // n_run, (n_chunks * (p_run + 1)) // n_run, per_tile(par_e),
            row_token)
    meta = tuple(m.astype(jnp.int32) for m in meta)

    ys = _experts(xs, meta, wg, wu, wd, layer, tm)
    shared = _shared_ffn(xb, sg.astype(_BF16), su.astype(_BF16), sd.astype(_BF16))
    return _combine(pos.reshape(-1), ys, w_t.T, shared, x, g, b)


def _fgate_kernel(x_ref, wt_ref, b_ref, c_ref, carry_ref):
    s = pl.program_id(1)

    @pl.when(s == 0)
    def _():
        carry_ref[...] = jnp.zeros_like(carry_ref)

    ts = x_ref.shape[0]
    z = lax.dot_general(wt_ref[...], x_ref[...], (((1,), (1,)), ((), ())),
                        precision=lax.Precision.HIGHEST,
                        preferred_element_type=_F32) + b_ref[...]
    ls = jnp.minimum(z, 0.0) - jnp.log1p(jnp.exp(-jnp.abs(z)))
    r_i = lax.broadcasted_iota(jnp.int32, (ts, ts), 0)
    c_i = lax.broadcasted_iota(jnp.int32, (ts, ts), 1)
    upto = jnp.where(r_i <= c_i, 1.0, 0.0)
    c = carry_ref[...] + jnp.dot(ls, upto, precision=lax.Precision.HIGHEST,
                                 preferred_element_type=_F32)
    c_ref[...] = c
    carry_ref[...] = c[:, ts - 1:ts]


def _fgate(x3, w_f, b_f, ts=512):
    bsz, s, d = x3.shape
    return pl.pallas_call(
        _fgate_kernel,
        grid=(bsz, s // ts),
        in_specs=[pl.BlockSpec((None, ts, d), lambda bb, i: (bb, i, 0)),
                  pl.BlockSpec((N_HEADS, d), lambda bb, i: (0, 0)),
                  pl.BlockSpec((N_HEADS, 1), lambda bb, i: (0, 0))],
        out_specs=pl.BlockSpec((None, N_HEADS, ts), lambda bb, i: (bb, 0, i)),
        out_shape=jax.ShapeDtypeStruct((bsz, N_HEADS, s), _F32),
        scratch_shapes=[pltpu.VMEM((N_HEADS, 1), _F32)],
        compiler_params=_params(2),
        name="fgate",
    )(x3, w_f.T, b_f.astype(_F32).reshape(N_HEADS, 1))


def _attn_kernel(q_ref, k_ref, v_ref, cq_ref, ck_ref, o_ref, *, tq, G):
    hp = pl.program_id(1)
    qi = pl.program_id(2)
    qk_scale = HEAD_DIM ** -0.5 * _LOG2E
    lane = lax.broadcasted_iota(jnp.int32, cq_ref.shape, 1)
    cqs = [_LOG2E * jnp.sum(jnp.where(lane == hp * G + g, cq_ref[...], 0.0), axis=1, keepdims=True)
           for g in range(G)]
    qs = [q_ref[:, g * HEAD_DIM:(g + 1) * HEAD_DIM] for g in range(G)]

    def step(j, carry, diagonal):
        k0 = pl.multiple_of(j * tq, tq)
        out = []
        for g in range(G):
            m, l, acc = carry[g]
            kj = k_ref[pl.ds(k0, tq), g * HEAD_DIM:(g + 1) * HEAD_DIM]
            vj = v_ref[pl.ds(k0, tq), g * HEAD_DIM:(g + 1) * HEAD_DIM]
            ck = _LOG2E * ck_ref[g:g + 1, pl.ds(k0, tq)]
            s = lax.dot_general(qs[g], kj, (((1,), (1,)), ((), ())), preferred_element_type=_F32)
            s = s * qk_scale - ck
            if diagonal:
                r_i = lax.broadcasted_iota(jnp.int32, s.shape, 0)
                c_i = lax.broadcasted_iota(jnp.int32, s.shape, 1)
                s = jnp.where(c_i <= r_i, s, _NEG)
            m_new = jnp.maximum(m, jnp.max(s, axis=1, keepdims=True) + cqs[g])
            alpha = jnp.exp2(m - m_new)
            p = jnp.exp2(s - (m_new - cqs[g]))
            l = alpha * l + jnp.sum(p, axis=1, keepdims=True)
            acc = alpha * acc + jnp.dot(p.astype(_BF16), vj, preferred_element_type=_F32)
            out.append((m_new, l, acc))
        return tuple(out)

    init = tuple((jnp.full((tq, 1), _NEG, _F32), jnp.zeros((tq, 1), _F32),
                  jnp.zeros((tq, HEAD_DIM), _F32)) for _ in range(G))
    carry = lax.fori_loop(0, qi, lambda j, c: step(j, c, False), init)
    fin = step(qi, carry, True)
    for g in range(G):
        _, l, acc = fin[g]
        o_ref[:, g * HEAD_DIM:(g + 1) * HEAD_DIM] = (acc / l).astype(o_ref.dtype)


def _attention(q3, kv3, c_sh, c_hs, tq=1024, G=_HEADS_PER_STEP):
    bsz, s, d = q3.shape
    w = G * HEAD_DIM
    nh = N_HEADS // G
    return pl.pallas_call(
        functools.partial(_attn_kernel, tq=tq, G=G),
        grid=(bsz, nh, s // tq),
        in_specs=[pl.BlockSpec((None, tq, w), lambda bb, h, i: (bb, i, h)),
                  pl.BlockSpec((None, s, w), lambda bb, h, i: (bb, 0, h)),
                  pl.BlockSpec((None, s, w), lambda bb, h, i: (bb, 0, nh + h)),
                  pl.BlockSpec((None, tq, N_HEADS), lambda bb, h, i: (bb, i, 0)),
                  pl.BlockSpec((None, None, G, s), lambda bb, h, i: (bb, h, 0, 0))],
        out_specs=pl.BlockSpec((None, tq, w), lambda bb, h, i: (bb, i, h)),
        out_shape=jax.ShapeDtypeStruct((bsz, s, d), _BF16),
        compiler_params=_params(3),
        name="fox_attention",
    )(q3, kv3, kv3, c_sh, c_hs.reshape(bsz, nh, G, s))


def kernel(x, ln_g, ln_b, conv_w_in, conv_w, conv_w_out, kv_w, kv_fb, attn_w_q, attn_w_o,
           router_w, router_b, exp_w_gate, exp_w_up, exp_w_down,
           shared_w_gate, shared_w_up, shared_w_down):
    bsz, seq, d = x.shape
    t = bsz * seq
    xf = x.reshape(t, d)
    xb = xf.astype(_BF16)
    kv3 = c_sh = c_hs = None
    for l in range(DEPTH):
        if l < N_A_LAYERS:
            y = _conv_in(xb, conv_w_in, conv_w, l, seq)
            h = _mm(y, conv_w_out, l, d, _F32)
        else:
            j = l - N_A_LAYERS
            if l == N_A_LAYERS:
                kv = _mm(xb, kv_w, None, 2 * d, _BF16)
                kv3 = kv.reshape(bsz, seq, 2 * d)
                c_hs = _fgate(xf.reshape(bsz, seq, d), kv_w[:, 2 * d:], kv_fb)
                c_sh = jnp.transpose(c_hs, (0, 2, 1))
            q = _mm(xb, attn_w_q, j, d, _BF16)
            o = _attention(q.reshape(bsz, seq, d), kv3, c_sh, c_hs)
            h = _mm(o.reshape(t, d), attn_w_o, j, d, _F32)
        xf, xb, xs = _res_ln(xf, h, ln_g[l, 0], ln_b[l, 0])
        xf, xb, xs = _moe_block(xf, xb, xs, l, router_w[l], router_b[l],
                                exp_w_gate, exp_w_up, exp_w_down,
                                shared_w_gate[l], shared_w_up[l], shared_w_down[l],
                                ln_g[l, 1], ln_b[l, 1])
    return xf.reshape(bsz, seq, d)
```

```python
import functools
import math

import jax
import jax.numpy as jnp
from jax import lax
from jax.experimental import pallas as pl
from jax.experimental.pallas import tpu as pltpu

D_MODEL = 4096
DEPTH = 2
N_A_LAYERS = DEPTH // 2
HEAD_DIM = 128
N_HEADS = D_MODEL // HEAD_DIM
CONV_WIDTH = 3
N_EXPERTS = 64
N_GROUPS = 8
GROUP_SIZE = N_EXPERTS // N_GROUPS
TOPK_GROUPS = 4
TOP_K = 8
D_EXPERT = 384
ROUTED_SCALE = 2.5
ALPHA = (2.0 * DEPTH) ** 0.25
LN_EPS = 1e-5

_MIB = 1024 * 1024
_VMEM_LIMIT = 56 * _MIB
_NEG = -1e30
_LOG2E = math.log2(math.e)
_ISSUE_UNROLL = 8
_W_CHUNKS = 4
_LANES = 128
_HEADS_PER_STEP = 2

_F32 = jnp.float32
_BF16 = jnp.bfloat16


def _params(n_axes, vmem=_VMEM_LIMIT):
    return pltpu.CompilerParams(dimension_semantics=("arbitrary",) * n_axes,
                                vmem_limit_bytes=vmem)


def _layer_norm_rows(v, g, b):
    mu = jnp.mean(v, axis=-1, keepdims=True)
    xc = v - mu
    var = jnp.mean(xc * xc, axis=-1, keepdims=True)
    return xc * lax.rsqrt(var + LN_EPS) * g + b


def _silu_mul(g, u):
    return g * jax.nn.sigmoid(g) * u


def _to_slab(v):
    return v.reshape(v.shape[0], v.shape[1] // _LANES, _LANES)


def _from_slab(v):
    return v.reshape(v.shape[0], v.shape[1] * v.shape[2])


def _mm_kernel(a_ref, w_ref, o_ref, wb_ref):
    @pl.when(pl.program_id(1) == 0)
    def _():
        wb_ref[...] = w_ref[...].astype(_BF16)

    o_ref[...] = jnp.dot(a_ref[...], wb_ref[...],
                         preferred_element_type=_F32).astype(o_ref.dtype)


def _mm(a, w, layer, n, out_dtype, tm=1024, tn=512):
    m, k = a.shape
    if layer is None:
        w_spec = pl.BlockSpec((k, tn), lambda j, i: (0, j))
    else:
        w_spec = pl.BlockSpec((None, k, tn), lambda j, i: (layer, 0, j))
    return pl.pallas_call(
        _mm_kernel,
        grid=(n // tn, m // tm),
        in_specs=[pl.BlockSpec((tm, k), lambda j, i: (i, 0)), w_spec],
        out_specs=pl.BlockSpec((tm, tn), lambda j, i: (i, j)),
        out_shape=jax.ShapeDtypeStruct((m, n), out_dtype),
        scratch_shapes=[pltpu.VMEM((k, tn), _BF16)],
        compiler_params=_params(2),
        name="dense_mm",
    )(a, w)


def _conv_in_kernel(a_ref, wb_ref, wc_ref, wu_ref, cw_ref, y_ref, wbf_ref, carry_ref,
                    *, tiles_per_seq):
    i = pl.program_id(1)

    @pl.when(i == 0)
    def _():
        wbf_ref[0] = wb_ref[...].astype(_BF16)
        wbf_ref[1] = wc_ref[...].astype(_BF16)
        wbf_ref[2] = wu_ref[...].astype(_BF16)

    @pl.when(i % tiles_per_seq == 0)
    def _():
        carry_ref[...] = jnp.zeros_like(carry_ref)

    a = a_ref[...]
    gate_out = jnp.dot(a, wbf_ref[0], preferred_element_type=_F32)
    gate_in = jnp.dot(a, wbf_ref[1], preferred_element_type=_F32)
    u = jnp.dot(a, wbf_ref[2], preferred_element_type=_F32)
    z = gate_in * u
    tm = z.shape[0]
    row = lax.broadcasted_iota(jnp.int32, z.shape, 0)
    p1 = carry_ref[7:8, :]
    p2 = carry_ref[6:7, :]
    z1 = jnp.where(row == 0, p1, pltpu.roll(z, 1, 0))
    z2 = jnp.where(row == 0, p2, jnp.where(row == 1, p1, pltpu.roll(z, 2, 0)))
    cw = cw_ref[...]
    conv = cw[0:1, :] * z2 + cw[1:2, :] * z1 + cw[2:3, :] * z
    y_ref[...] = (gate_out * conv).astype(y_ref.dtype)
    carry_ref[...] = z[tm - 8:tm, :]


def _conv_in(a, w_in, conv_w, layer, seq, tm=1024, tn=256):
    m, k = a.shape
    d = w_in.shape[2] // 3
    nb = d // tn
    kern = functools.partial(_conv_in_kernel, tiles_per_seq=seq // tm)
    return pl.pallas_call(
        kern,
        grid=(nb, m // tm),
        in_specs=[pl.BlockSpec((tm, k), lambda j, i: (i, 0)),
                  pl.BlockSpec((None, k, tn), lambda j, i: (layer, 0, j)),
                  pl.BlockSpec((None, k, tn), lambda j, i: (layer, 0, j + nb)),
                  pl.BlockSpec((None, k, tn), lambda j, i: (layer, 0, j + 2 * nb)),
                  pl.BlockSpec((None, CONV_WIDTH, tn), lambda j, i: (layer, 0, j))],
        out_specs=pl.BlockSpec((tm, tn), lambda j, i: (i, j)),
        out_shape=jax.ShapeDtypeStruct((m, d), _BF16),
        scratch_shapes=[pltpu.VMEM((3, k, tn), _BF16), pltpu.VMEM((8, tn), _F32)],
        compiler_params=_params(2),
        name="conv_in",
    )(a, w_in, w_in, w_in, conv_w)


def _res_ln_kernel(x_ref, h_ref, g_ref, b_ref, o_ref, ob_ref, os_ref):
    v = ALPHA * x_ref[...] + h_ref[...]
    y = _layer_norm_rows(v, g_ref[...], b_ref[...])
    yb = y.astype(_BF16)
    o_ref[...] = y
    ob_ref[...] = yb
    os_ref[...] = _to_slab(yb)


def _res_ln(x, h, g, b, tm=256):
    m, d = x.shape
    row = pl.BlockSpec((tm, d), lambda i: (i, 0))
    vec = pl.BlockSpec((1, d), lambda i: (0, 0))
    return pl.pallas_call(
        _res_ln_kernel,
        grid=(m // tm,),
        in_specs=[row, row, vec, vec],
        out_specs=[row, row, pl.BlockSpec((tm, d // _LANES, _LANES), lambda i: (i, 0, 0))],
        out_shape=[jax.ShapeDtypeStruct((m, d), _F32), jax.ShapeDtypeStruct((m, d), _BF16),
                   jax.ShapeDtypeStruct((m, d // _LANES, _LANES), _BF16)],
        compiler_params=_params(1),
        name="res_ln",
    )(x, h, g.reshape(1, d), b.reshape(1, d))


def _first_index_of_max(vals, axis, size):
    mx = jnp.max(vals, axis=axis, keepdims=True)
    idx = lax.broadcasted_iota(jnp.int32, vals.shape, axis)
    first = jnp.min(jnp.where(vals == mx, idx, size), axis=axis, keepdims=True)
    return mx, idx == first


def _router_kernel(x_ref, wt_ref, b_ref, idx_ref, w_ref, rank_ref, cnt_ref, carry_ref):
    i = pl.program_id(0)

    @pl.when(i == 0)
    def _():
        carry_ref[...] = jnp.zeros_like(carry_ref)

    tm = x_ref.shape[0]
    logits = lax.dot_general(wt_ref[...], x_ref[...], (((1,), (1,)), ((), ())),
                             precision=lax.Precision.HIGHEST,
                             preferred_element_type=_F32)
    scores = jax.nn.sigmoid(logits)
    choice = scores + b_ref[...]
    c3 = choice.reshape(N_GROUPS, GROUP_SIZE, tm)
    m1, hit1 = _first_index_of_max(c3, 1, GROUP_SIZE)
    m2 = jnp.max(jnp.where(hit1, _NEG, c3), axis=1, keepdims=True)
    cur = (m1 + m2).reshape(N_GROUPS, tm)
    sel = jnp.zeros((N_GROUPS, tm), _F32)
    for _ in range(TOPK_GROUPS):
        _, hit = _first_index_of_max(cur, 0, N_GROUPS)
        sel = jnp.where(hit, 1.0, sel)
        cur = jnp.where(hit, _NEG, cur)
    exp_mask = jnp.broadcast_to(sel.reshape(N_GROUPS, 1, tm),
                                (N_GROUPS, GROUP_SIZE, tm)).reshape(N_EXPERTS, tm)
    masked = jnp.where(exp_mask > 0.0, choice, _NEG)

    eidx = lax.broadcasted_iota(jnp.int32, (N_EXPERTS, tm), 0)
    hits, idxs, ws = [], [], []
    member = jnp.zeros((N_EXPERTS, tm), _F32)
    for _ in range(TOP_K):
        _, hit = _first_index_of_max(masked, 0, N_EXPERTS)
        hits.append(hit)
        idxs.append(jnp.sum(jnp.where(hit, eidx, 0), axis=0, keepdims=True))
        ws.append(jnp.sum(jnp.where(hit, scores, 0.0), axis=0, keepdims=True))
        member = jnp.where(hit, 1.0, member)
        masked = jnp.where(hit, _NEG, masked)
    w = jnp.concatenate(ws, axis=0)
    w = w / jnp.sum(w, axis=0, keepdims=True) * ROUTED_SCALE
    idx_ref[...] = jnp.concatenate(idxs, axis=0)
    w_ref[...] = w

    r_i = lax.broadcasted_iota(jnp.int32, (tm, tm), 0)
    c_i = lax.broadcasted_iota(jnp.int32, (tm, tm), 1)
    before = jnp.where(r_i < c_i, 1.0, 0.0).astype(_BF16)
    prefix = jnp.dot(member.astype(_BF16), before, preferred_element_type=_F32)
    base = carry_ref[...] + prefix
    ranks = [jnp.sum(jnp.where(h, base, 0.0), axis=0, keepdims=True) for h in hits]
    rank_ref[...] = jnp.concatenate(ranks, axis=0).astype(jnp.int32)
    total = carry_ref[...] + jnp.sum(member, axis=1, keepdims=True)
    carry_ref[...] = total
    cnt_ref[...] = total.astype(jnp.int32)


def _router(x, w_r, b_r, tm=512):
    t, d = x.shape
    kt = pl.BlockSpec((TOP_K, tm), lambda i: (0, i))
    return pl.pallas_call(
        _router_kernel,
        grid=(t // tm,),
        in_specs=[pl.BlockSpec((tm, d), lambda i: (i, 0)),
                  pl.BlockSpec((N_EXPERTS, d), lambda i: (0, 0)),
                  pl.BlockSpec((N_EXPERTS, 1), lambda i: (0, 0))],
        out_specs=[kt, kt, kt, pl.BlockSpec((N_EXPERTS, 1), lambda i: (0, 0))],
        out_shape=[jax.ShapeDtypeStruct((TOP_K, t), jnp.int32),
                   jax.ShapeDtypeStruct((TOP_K, t), _F32),
                   jax.ShapeDtypeStruct((TOP_K, t), jnp.int32),
                   jax.ShapeDtypeStruct((N_EXPERTS, 1), jnp.int32)],
        scratch_shapes=[pltpu.VMEM((N_EXPERTS, 1), _F32)],
        compiler_params=_params(1),
        name="router",
    )(x, w_r.T, b_r.astype(_F32).reshape(N_EXPERTS, 1))


def _experts_kernel(te_ref, nv_ref, nxt_ref, nn_ref, clo_ref, chi_ref, par_ref, tok_ref,
                    xs_hbm, wg_hbm, wu_hbm, wd_hbm, y_ref,
                    xbuf, wg_b, wu_b, wd_b, stg_a, stg_d, gsem, sem_a, sem_d, *, tm, layer):
    i = pl.program_id(0)
    n_valid = nv_ref[0]
    d = wg_hbm.shape[2]
    ka = d // _W_CHUNKS
    kd = D_EXPERT // _W_CHUNKS
    n_chunks = 3 * _W_CHUNKS

    def chunk_copy(c, e, w_hbm, rows, stg, sem):
        s2 = c % 2
        r0 = pl.multiple_of((c % _W_CHUNKS) * rows, rows)
        return pltpu.make_async_copy(w_hbm.at[layer, e, pl.ds(r0, rows), :], stg.at[s2], sem.at[s2])

    def start_chunk(c, e):
        @pl.when(c < _W_CHUNKS)
        def _():
            chunk_copy(c, e, wg_hbm, ka, stg_a, sem_a).start()

        @pl.when((c >= _W_CHUNKS) & (c < 2 * _W_CHUNKS))
        def _():
            chunk_copy(c, e, wu_hbm, ka, stg_a, sem_a).start()

        @pl.when(c >= 2 * _W_CHUNKS)
        def _():
            chunk_copy(c, e, wd_hbm, kd, stg_d, sem_d).start()

    def finish_chunk(c, e, slot):
        s2 = c % 2
        ra = pl.multiple_of((c % _W_CHUNKS) * ka, ka)
        rd = pl.multiple_of((c % _W_CHUNKS) * kd, kd)

        @pl.when(c < _W_CHUNKS)
        def _():
            chunk_copy(c, e, wg_hbm, ka, stg_a, sem_a).wait()
            wg_b[slot, pl.ds(ra, ka), :] = stg_a[s2].astype(_BF16)

        @pl.when((c >= _W_CHUNKS) & (c < 2 * _W_CHUNKS))
        def _():
            chunk_copy(c, e, wu_hbm, ka, stg_a, sem_a).wait()
            wu_b[slot, pl.ds(ra, ka), :] = stg_a[s2].astype(_BF16)

        @pl.when(c >= 2 * _W_CHUNKS)
        def _():
            chunk_copy(c, e, wd_hbm, kd, stg_d, sem_d).wait()
            wd_b[slot, pl.ds(rd, kd), :] = stg_d[s2].astype(_BF16)

    def load_chunks(lo, hi, e, slot):
        def body(c, carry):
            finish_chunk(c, e, slot)

            @pl.when(c + 2 < n_chunks)
            def _():
                start_chunk(c + 2, e)
            return carry
        lax.fori_loop(lo, hi, body, 0)

    def start_gather(tile, slot):
        base = tile * tm

        def body(r8, c):
            for u in range(_ISSUE_UNROLL):
                r = r8 * _ISSUE_UNROLL + u
                tok = tok_ref[base + r]
                pltpu.make_async_copy(xs_hbm.at[tok], xbuf.at[slot, r], gsem.at[slot]).start()
            return c
        lax.fori_loop(0, tm // _ISSUE_UNROLL, body, 0)

    @pl.when(i == 0)
    def _():
        start_gather(0, 0)
        start_chunk(0, te_ref[0])
        start_chunk(1, te_ref[0])
        load_chunks(0, n_chunks, te_ref[0], par_ref[0])

        @pl.when(nxt_ref[0] >= 0)
        def _():
            start_chunk(0, nxt_ref[0])
            start_chunk(1, nxt_ref[0])

    @pl.when(i + 1 < n_valid)
    def _():
        start_gather(i + 1, (i + 1) % 2)

    @pl.when(i < n_valid)
    def _():
        slot = par_ref[i]
        nxt = nxt_ref[i]
        lo = clo_ref[i]
        hi = chi_ref[i]
        third = (hi - lo + 2) // 3
        m1 = jnp.minimum(lo + third, hi)
        m2 = jnp.minimum(m1 + third, hi)

        def prefetch(c0, c1):
            @pl.when(nxt >= 0)
            def _():
                load_chunks(c0, c1, nxt, 1 - slot)

        prefetch(lo, m1)
        gslot = i % 2
        pltpu.make_async_copy(xs_hbm.at[pl.ds(0, tm)], xbuf.at[gslot], gsem.at[gslot]).wait()
        a = _from_slab(xbuf[gslot])
        g = jnp.dot(a, wg_b[slot], preferred_element_type=_F32)
        u = jnp.dot(a, wu_b[slot], preferred_element_type=_F32)
        prefetch(m1, m2)
        h = _silu_mul(g, u).astype(_BF16)
        y = jnp.dot(h, wd_b[slot], preferred_element_type=_F32)
        y_ref[...] = _to_slab(y.astype(_BF16))
        prefetch(m2, hi)

        @pl.when((nxt >= 0) & (hi == n_chunks) & (nn_ref[i] >= 0))
        def _():
            start_chunk(0, nn_ref[i])
            start_chunk(1, nn_ref[i])

    @pl.when(i >= n_valid)
    def _():
        y_ref[...] = jnp.zeros_like(y_ref)


def _experts(xs, meta, wg, wu, wd, layer, tm):
    t, sl, ln = xs.shape
    d = sl * ln
    n_tiles = meta[0].shape[0]
    ka = d // _W_CHUNKS
    kd = D_EXPERT // _W_CHUNKS
    any_spec = pl.BlockSpec(memory_space=pl.ANY)
    grid_spec = pltpu.PrefetchScalarGridSpec(
        num_scalar_prefetch=len(meta),
        grid=(n_tiles,),
        in_specs=[any_spec, any_spec, any_spec, any_spec],
        out_specs=pl.BlockSpec((tm, sl, ln), lambda i, *_: (i, 0, 0)),
        scratch_shapes=[pltpu.VMEM((2, tm, sl, ln), _BF16),
                        pltpu.VMEM((2, d, D_EXPERT), _BF16),
                        pltpu.VMEM((2, d, D_EXPERT), _BF16),
                        pltpu.VMEM((2, D_EXPERT, d), _BF16),
                        pltpu.VMEM((2, ka, D_EXPERT), _F32),
                        pltpu.VMEM((2, kd, d), _F32),
                        pltpu.SemaphoreType.DMA((2,)),
                        pltpu.SemaphoreType.DMA((2,)),
                        pltpu.SemaphoreType.DMA((2,))],
    )
    return pl.pallas_call(
        functools.partial(_experts_kernel, tm=tm, layer=layer),
        grid_spec=grid_spec,
        out_shape=jax.ShapeDtypeStruct((n_tiles * tm, sl, ln), _BF16),
        compiler_params=_params(1),
        name="experts",
    )(*meta, xs, wg, wu, wd)


def _ffn_kernel(a_ref, wg_ref, wu_ref, wd_ref, o_ref):
    a = a_ref[...]
    g = jnp.dot(a, wg_ref[...], preferred_element_type=_F32)
    u = jnp.dot(a, wu_ref[...], preferred_element_type=_F32)
    h = _silu_mul(g, u).astype(_BF16)
    o_ref[...] = jnp.dot(h, wd_ref[...], preferred_element_type=_F32)


def _shared_ffn(a, wg, wu, wd, tm=512):
    m, d = a.shape
    f = wg.shape[1]
    return pl.pallas_call(
        _ffn_kernel,
        grid=(m // tm,),
        in_specs=[pl.BlockSpec((tm, d), lambda i: (i, 0)),
                  pl.BlockSpec((d, f), lambda i: (0, 0)),
                  pl.BlockSpec((d, f), lambda i: (0, 0)),
                  pl.BlockSpec((f, d), lambda i: (0, 0))],
        out_specs=pl.BlockSpec((tm, d), lambda i: (i, 0)),
        out_shape=jax.ShapeDtypeStruct((m, d), _F32),
        compiler_params=_params(1),
        name="shared_ffn",
    )(a, wg, wu, wd)


def _combine_kernel(pos_ref, y_hbm, w_ref, sh_ref, x_ref, g_ref, b_ref, o_ref, ob_ref, os_ref,
                    ybuf, sem, *, tc, n_tok):
    i = pl.program_id(0)
    n_steps = pl.num_programs(0)

    def start_gather(step, slot):
        for k in range(TOP_K):
            base = k * n_tok + step * tc

            def body(r8, c, k=k, base=base):
                for u in range(_ISSUE_UNROLL):
                    r = r8 * _ISSUE_UNROLL + u
                    p = pos_ref[base + r]
                    pltpu.make_async_copy(y_hbm.at[p], ybuf.at[slot, k, r], sem.at[slot]).start()
                return c
            lax.fori_loop(0, tc // _ISSUE_UNROLL, body, 0)

    @pl.when(i == 0)
    def _():
        start_gather(0, 0)

    @pl.when(i + 1 < n_steps)
    def _():
        start_gather(i + 1, (i + 1) % 2)

    slot = i % 2
    for k in range(TOP_K):
        pltpu.make_async_copy(y_hbm.at[pl.ds(0, tc)], ybuf.at[slot, k], sem.at[slot]).wait()
    w = w_ref[...]
    acc = jnp.zeros(ybuf.shape[2:], _F32)
    for k in range(TOP_K):
        acc = acc + w[:, k:k + 1].reshape(tc, 1, 1) * ybuf[slot, k].astype(_F32)
    v = ALPHA * x_ref[...] + (_from_slab(acc) + sh_ref[...])
    y = _layer_norm_rows(v, g_ref[...], b_ref[...])
    yb = y.astype(_BF16)
    o_ref[...] = y
    ob_ref[...] = yb
    os_ref[...] = _to_slab(yb)


def _combine(pos_flat, ys, w_tok, shared, x, g, b, tc=64):
    t, d = x.shape
    sl = d // _LANES
    row = pl.BlockSpec((tc, d), lambda i, pos: (i, 0))
    slab = pl.BlockSpec((tc, sl, _LANES), lambda i, pos: (i, 0, 0))
    vec = pl.BlockSpec((1, d), lambda i, pos: (0, 0))
    grid_spec = pltpu.PrefetchScalarGridSpec(
        num_scalar_prefetch=1,
        grid=(t // tc,),
        in_specs=[pl.BlockSpec(memory_space=pl.ANY),
                  pl.BlockSpec((tc, TOP_K), lambda i, pos: (i, 0)),
                  row, row, vec, vec],
        out_specs=[row, row, slab],
        scratch_shapes=[pltpu.VMEM((2, TOP_K, tc, sl, _LANES), _BF16),
                        pltpu.SemaphoreType.DMA((2,))],
    )
    return pl.pallas_call(
        functools.partial(_combine_kernel, tc=tc, n_tok=t),
        grid_spec=grid_spec,
        out_shape=[jax.ShapeDtypeStruct((t, d), _F32), jax.ShapeDtypeStruct((t, d), _BF16),
                   jax.ShapeDtypeStruct((t, sl, _LANES), _BF16)],
        compiler_params=_params(1),
        name="combine",
    )(pos_flat, ys, w_tok, shared, x, g.reshape(1, d), b.reshape(1, d))


def _row_table_kernel(pos_ref, lo_ref, hi_ref, tab_ref, *, n_tok):
    def clear_expert(e, c):
        def clear(r, c2):
            tab_ref[r] = 0
            return c2
        lax.fori_loop(lo_ref[e], hi_ref[e], clear, 0)
        return c
    lax.fori_loop(0, lo_ref.shape[0], clear_expert, 0)

    for k in range(TOP_K):
        def fill(i, c, k=k):
            for u in range(_ISSUE_UNROLL):
                tkn = i * _ISSUE_UNROLL + u
                tab_ref[pos_ref[k * n_tok + tkn]] = tkn
            return c
        lax.fori_loop(0, n_tok // _ISSUE_UNROLL, fill, 0)


def _row_table(pos_flat, pad_lo, pad_hi, n_rows, n_tok):
    smem = pl.BlockSpec(memory_space=pltpu.SMEM)
    return pl.pallas_call(
        functools.partial(_row_table_kernel, n_tok=n_tok),
        in_specs=[smem, smem, smem],
        out_specs=smem,
        out_shape=jax.ShapeDtypeStruct((n_rows,), jnp.int32),
        name="row_table",
    )(pos_flat, pad_lo, pad_hi)


def _moe_block(x, xb, xs, layer, w_r, b_r, wg, wu, wd, sg, su, sd, g, b, tm=256):
    t, d = x.shape
    idx_t, w_t, rank_t, cnt = _router(x, w_r, b_r)

    counts = cnt[:, 0]
    tiles_e = (counts + tm - 1) // tm
    tile_end = jnp.cumsum(tiles_e)
    tile_start = tile_end - tiles_e
    row_off = tile_start * tm
    n_valid = tile_end[-1:]
    n_tiles = t * TOP_K // tm + N_EXPERTS
    e_ids = jnp.arange(N_EXPERTS, dtype=jnp.int32)
    pos = jnp.sum(jnp.where(idx_t[:, :, None] == e_ids, row_off, 0), axis=-1) + rank_t
    pad_hi = jnp.concatenate([row_off[1:], jnp.full((1,), n_tiles * tm, jnp.int32)])
    row_token = _row_table(pos.reshape(-1), row_off + counts, pad_hi, n_tiles * tm, t)
    tile_ids = jnp.minimum(jnp.arange(n_tiles, dtype=jnp.int32), n_valid - 1)
    onehot_te = (tile_end[None, :] <= tile_ids[:, None]).astype(jnp.int32)
    tile_expert = jnp.sum(onehot_te, axis=1)

    has = tiles_e > 0
    later = (e_ids[None, :] > e_ids[:, None]) & has[None, :]
    nxt_e = jnp.min(jnp.where(later, e_ids[None, :], N_EXPERTS), axis=1)
    nxt_e = jnp.where(nxt_e == N_EXPERTS, -1, nxt_e)
    nxt_onehot = nxt_e[:, None] == e_ids[None, :]
    nn_e = jnp.where(nxt_e >= 0, jnp.sum(jnp.where(nxt_onehot, nxt_e[None, :], 0), axis=1), -1)
    par_e = (jnp.cumsum(has.astype(jnp.int32)) - has.astype(jnp.int32)) % 2

    def per_tile(v):
        return jnp.sum(jnp.where(tile_expert[:, None] == e_ids[None, :], v[None, :], 0), axis=1)
    n_run = jnp.maximum(per_tile(tiles_e), 1)
    p_run = tile_ids - per_tile(tile_start)
    n_chunks = 3 * _W_CHUNKS
    meta = (tile_expert, n_valid.astype(jnp.int32), per_tile(nxt_e), per_tile(nn_e),
            (n_chunks * p_run) // n_run, (n_chunks * (p_run + 1)) // n_run, per_tile(par_e),
            row_token)
    meta = tuple(m.astype(jnp.int32) for m in meta)

    ys = _experts(xs, meta, wg, wu, wd, layer, tm)
    shared = _shared_ffn(xb, sg.astype(_BF16), su.astype(_BF16), sd.astype(_BF16))
    return _combine(pos.reshape(-1), ys, w_t.T, shared, x, g, b)


def _fgate_kernel(x_ref, wt_ref, b_ref, c_ref, carry_ref):
    s = pl.program_id(1)

    @pl.when(s == 0)
    def _():
        carry_ref[...] = jnp.zeros_like(carry_ref)

    ts = x_ref.shape[0]
    z = lax.dot_general(wt_ref[...], x_ref[...], (((1,), (1,)), ((), ())),
                        precision=lax.Precision.HIGHEST,
                        preferred_element_type=_F32) + b_ref[...]
    ls = jnp.minimum(z, 0.0) - jnp.log1p(jnp.exp(-jnp.abs(z)))
    r_i = lax.broadcasted_iota(jnp.int32, (ts, ts), 0)
    c_i = lax.broadcasted_iota(jnp.int32, (ts, ts), 1)
    upto = jnp.where(r_i <= c_i, 1.0, 0.0)
    c = carry_ref[...] + jnp.dot(ls, upto, precision=lax.Precision.HIGHEST,
                                 preferred_element_type=_F32)
    c_ref[...] = c
    carry_ref[...] = c[:, ts - 1:ts]


def _fgate(x3, w_f, b_f, ts=512):
    bsz, s, d = x3.shape
    return pl.pallas_call(
        _fgate_kernel,
        grid=(bsz, s // ts),
        in_specs=[pl.BlockSpec((None, ts, d), lambda bb, i: (bb, i, 0)),
                  pl.BlockSpec((N_HEADS, d), lambda bb, i: (0, 0)),
                  pl.BlockSpec((N_HEADS, 1), lambda bb, i: (0, 0))],
        out_specs=pl.BlockSpec((None, N_HEADS, ts), lambda bb, i: (bb, 0, i)),
        out_shape=jax.ShapeDtypeStruct((bsz, N_HEADS, s), _F32),
        scratch_shapes=[pltpu.VMEM((N_HEADS, 1), _F32)],
        compiler_params=_params(2),
        name="fgate",
    )(x3, w_f.T, b_f.astype(_F32).reshape(N_HEADS, 1))


def _attn_kernel(q_ref, k_ref, v_ref, cq_ref, ck_ref, o_ref, *, tq, G):
    hp = pl.program_id(1)
    qi = pl.program_id(2)
    qk_scale = HEAD_DIM ** -0.5 * _LOG2E
    lane = lax.broadcasted_iota(jnp.int32, cq_ref.shape, 1)
    cqs = [_LOG2E * jnp.sum(jnp.where(lane == hp * G + g, cq_ref[...], 0.0), axis=1, keepdims=True)
           for g in range(G)]
    qs = [q_ref[:, g * HEAD_DIM:(g + 1) * HEAD_DIM] for g in range(G)]

    def step(j, carry, diagonal):
        k0 = pl.multiple_of(j * tq, tq)
        out = []
        for g in range(G):
            m, l, acc = carry[g]
            kj = k_ref[pl.ds(k0, tq), g * HEAD_DIM:(g + 1) * HEAD_DIM]
            vj = v_ref[pl.ds(k0, tq), g * HEAD_DIM:(g + 1) * HEAD_DIM]
            ck = _LOG2E * ck_ref[g:g + 1, pl.ds(k0, tq)]
            s = lax.dot_general(qs[g], kj, (((1,), (1,)), ((), ())), preferred_element_type=_F32)
            s = s * qk_scale - ck
            if diagonal:
                r_i = lax.broadcasted_iota(jnp.int32, s.shape, 0)
                c_i = lax.broadcasted_iota(jnp.int32, s.shape, 1)
                s = jnp.where(c_i <= r_i, s, _NEG)
            m_new = jnp.maximum(m, jnp.max(s, axis=1, keepdims=True) + cqs[g])
            alpha = jnp.exp2(m - m_new)
            p = jnp.exp2(s - (m_new - cqs[g]))
            l = alpha * l + jnp.sum(p, axis=1, keepdims=True)
            acc = alpha * acc + jnp.dot(p.astype(_BF16), vj, preferred_element_type=_F32)
            out.append((m_new, l, acc))
        return tuple(out)

    init = tuple((jnp.full((tq, 1), _NEG, _F32), jnp.zeros((tq, 1), _F32),
                  jnp.zeros((tq, HEAD_DIM), _F32)) for _ in range(G))
    carry = lax.fori_loop(0, qi, lambda j, c: step(j, c, False), init)
    fin = step(qi, carry, True)
    for g in range(G):
        _, l, acc = fin[g]
        o_ref[:, g * HEAD_DIM:(g + 1) * HEAD_DIM] = (acc / l).astype(o_ref.dtype)


def _attention(q3, kv3, c_sh, c_hs, tq=1024, G=_HEADS_PER_STEP):
    bsz, s, d = q3.shape
    w = G * HEAD_DIM
    nh = N_HEADS // G
    return pl.pallas_call(
        functools.partial(_attn_kernel, tq=tq, G=G),
        grid=(bsz, nh, s // tq),
        in_specs=[pl.BlockSpec((None, tq, w), lambda bb, h, i: (bb, i, h)),
                  pl.BlockSpec((None, s, w), lambda bb, h, i: (bb, 0, h)),
                  pl.BlockSpec((None, s, w), lambda bb, h, i: (bb, 0, nh + h)),
                  pl.BlockSpec((None, tq, N_HEADS), lambda bb, h, i: (bb, i, 0)),
                  pl.BlockSpec((None, None, G, s), lambda bb, h, i: (bb, h, 0, 0))],
        out_specs=pl.BlockSpec((None, tq, w), lambda bb, h, i: (bb, i, h)),
        out_shape=jax.ShapeDtypeStruct((bsz, s, d), _BF16),
        compiler_params=_params(3),
        name="fox_attention",
    )(q3, kv3, kv3, c_sh, c_hs.reshape(bsz, nh, G, s))


def kernel(x, ln_g, ln_b, conv_w_in, conv_w, conv_w_out, kv_w, kv_fb, attn_w_q, attn_w_o,
           router_w, router_b, exp_w_gate, exp_w_up, exp_w_down,
           shared_w_gate, shared_w_up, shared_w_down):
    bsz, seq, d = x.shape
    t = bsz * seq
    xf = x.reshape(t, d)
    xb = xf.astype(_BF16)
    kv3 = c_sh = c_hs = None
    for l in range(DEPTH):
        if l < N_A_LAYERS:
            y = _conv_in(xb, conv_w_in, conv_w, l, seq)
            h = _mm(y, conv_w_out, l, d, _F32)
        else:
            j = l - N_A_LAYERS
            if l == N_A_LAYERS:
                kv = _mm(xb, kv_w, None, 2 * d, _BF16)
                kv3 = kv.reshape(bsz, seq, 2 * d)
                c_hs = _fgate(xf.reshape(bsz, seq, d), kv_w[:, 2 * d:], kv_fb)
                c_sh = jnp.transpose(c_hs, (0, 2, 1))
            q = _mm(xb, attn_w_q, j, d, _BF16)
            o = _attention(q.reshape(bsz, seq, d), kv3, c_sh, c_hs)
            h = _mm(o.reshape(t, d), attn_w_o, j, d, _F32)
        xf, xb, xs = _res_ln(xf, h, ln_g[l, 0], ln_b[l, 0])
        xf, xb, xs = _moe_block(xf, xb, xs, l, router_w[l], router_b[l],
                                exp_w_gate, exp_w_up, exp_w_down,
                                shared_w_gate[l], shared_w_up[l], shared_w_down[l],
                                ln_g[l, 1], ln_b[l, 1])
    return xf.reshape(bsz, seq, d)
```

```python
import functools
import math

import jax
import jax.numpy as jnp
from jax import lax
from jax.experimental import pallas as pl
from jax.experimental.pallas import tpu as pltpu

D_MODEL = 4096
DEPTH = 2
N_A_LAYERS = DEPTH // 2
HEAD_DIM = 128
N_HEADS = D_MODEL // HEAD_DIM
CONV_WIDTH = 3
N_EXPERTS = 64
N_GROUPS = 8
GROUP_SIZE = N_EXPERTS // N_GROUPS
TOPK_GROUPS = 4
TOP_K = 8
D_EXPERT = 384
ROUTED_SCALE = 2.5
ALPHA = (2.0 * DEPTH) ** 0.25
LN_EPS = 1e-5

_MIB = 1024 * 1024
_VMEM_LIMIT = 56 * _MIB
_NEG = -1e30
_LOG2E = math.log2(math.e)
_ISSUE_UNROLL = 8
_W_CHUNKS = 4
_LANES = 128
_HEADS_PER_STEP = 2

_F32 = jnp.float32
_BF16 = jnp.bfloat16


def _params(n_axes, vmem=_VMEM_LIMIT):
    return pltpu.CompilerParams(dimension_semantics=("arbitrary",) * n_axes,
                                vmem_limit_bytes=vmem)


def _layer_norm_rows(v, g, b):
    mu = jnp.mean(v, axis=-1, keepdims=True)
    xc = v - mu
    var = jnp.mean(xc * xc, axis=-1, keepdims=True)
    return xc * lax.rsqrt(var + LN_EPS) * g + b


def _silu_mul(g, u):
    return g * jax.nn.sigmoid(g) * u


def _to_slab(v):
    return v.reshape(v.shape[0], v.shape[1] // _LANES, _LANES)


def _from_slab(v):
    return v.reshape(v.shape[0], v.shape[1] * v.shape[2])


def _mm_kernel(a_ref, w_ref, o_ref, wb_ref):
    @pl.when(pl.program_id(1) == 0)
    def _():
        wb_ref[...] = w_ref[...].astype(_BF16)

    o_ref[...] = jnp.dot(a_ref[...], wb_ref[...],
                         preferred_element_type=_F32).astype(o_ref.dtype)


def _mm(a, w, layer, n, out_dtype, tm=1024, tn=512):
    m, k = a.shape
    if layer is None:
        w_spec = pl.BlockSpec((k, tn), lambda j, i: (0, j))
    else:
        w_spec = pl.BlockSpec((None, k, tn), lambda j, i: (layer, 0, j))
    return pl.pallas_call(
        _mm_kernel,
        grid=(n // tn, m // tm),
        in_specs=[pl.BlockSpec((tm, k), lambda j, i: (i, 0)), w_spec],
        out_specs=pl.BlockSpec((tm, tn), lambda j, i: (i, j)),
        out_shape=jax.ShapeDtypeStruct((m, n), out_dtype),
        scratch_shapes=[pltpu.VMEM((k, tn), _BF16)],
        compiler_params=_params(2),
        name="dense_mm",
    )(a, w)


def _conv_in_kernel(a_ref, wb_ref, wc_ref, wu_ref, cw_ref, y_ref, wbf_ref, carry_ref,
                    *, tiles_per_seq):
    i = pl.program_id(1)

    @pl.when(i == 0)
    def _():
        wbf_ref[0] = wb_ref[...].astype(_BF16)
        wbf_ref[1] = wc_ref[...].astype(_BF16)
        wbf_ref[2] = wu_ref[...].astype(_BF16)

    @pl.when(i % tiles_per_seq == 0)
    def _():
        carry_ref[...] = jnp.zeros_like(carry_ref)

    a = a_ref[...]
    gate_out = jnp.dot(a, wbf_ref[0], preferred_element_type=_F32)
    gate_in = jnp.dot(a, wbf_ref[1], preferred_element_type=_F32)
    u = jnp.dot(a, wbf_ref[2], preferred_element_type=_F32)
    z = gate_in * u
    tm = z.shape[0]
    row = lax.broadcasted_iota(jnp.int32, z.shape, 0)
    p1 = carry_ref[7:8, :]
    p2 = carry_ref[6:7, :]
    z1 = jnp.where(row == 0, p1, pltpu.roll(z, 1, 0))
    z2 = jnp.where(row == 0, p2, jnp.where(row == 1, p1, pltpu.roll(z, 2, 0)))
    cw = cw_ref[...]
    conv = cw[0:1, :] * z2 + cw[1:2, :] * z1 + cw[2:3, :] * z
    y_ref[...] = (gate_out * conv).astype(y_ref.dtype)
    carry_ref[...] = z[tm - 8:tm, :]


def _conv_in(a, w_in, conv_w, layer, seq, tm=1024, tn=256):
    m, k = a.shape
    d = w_in.shape[2] // 3
    nb = d // tn
    kern = functools.partial(_conv_in_kernel, tiles_per_seq=seq // tm)
    return pl.pallas_call(
        kern,
        grid=(nb, m // tm),
        in_specs=[pl.BlockSpec((tm, k), lambda j, i: (i, 0)),
                  pl.BlockSpec((None, k, tn), lambda j, i: (layer, 0, j)),
                  pl.BlockSpec((None, k, tn), lambda j, i: (layer, 0, j + nb)),
                  pl.BlockSpec((None, k, tn), lambda j, i: (layer, 0, j + 2 * nb)),
                  pl.BlockSpec((None, CONV_WIDTH, tn), lambda j, i: (layer, 0, j))],
        out_specs=pl.BlockSpec((tm, tn), lambda j, i: (i, j)),
        out_shape=jax.ShapeDtypeStruct((m, d), _BF16),
        scratch_shapes=[pltpu.VMEM((3, k, tn), _BF16), pltpu.VMEM((8, tn), _F32)],
        compiler_params=_params(2),
        name="conv_in",
    )(a, w_in, w_in, w_in, conv_w)


def _res_ln_kernel(x_ref, h_ref, g_ref, b_ref, o_ref, ob_ref, os_ref):
    v = ALPHA * x_ref[...] + h_ref[...]
    y = _layer_norm_rows(v, g_ref[...], b_ref[...])
    yb = y.astype(_BF16)
    o_ref[...] = y
    ob_ref[...] = yb
    os_ref[...] = _to_slab(yb)


def _res_ln(x, h, g, b, tm=256):
    m, d = x.shape
    row = pl.BlockSpec((tm, d), lambda i: (i, 0))
    vec = pl.BlockSpec((1, d), lambda i: (0, 0))
    return pl.pallas_call(
        _res_ln_kernel,
        grid=(m // tm,),
        in_specs=[row, row, vec, vec],
        out_specs=[row, row, pl.BlockSpec((tm, d // _LANES, _LANES), lambda i: (i, 0, 0))],
        out_shape=[jax.ShapeDtypeStruct((m, d), _F32), jax.ShapeDtypeStruct((m, d), _BF16),
                   jax.ShapeDtypeStruct((m, d // _LANES, _LANES), _BF16)],
        compiler_params=_params(1),
        name="res_ln",
    )(x, h, g.reshape(1, d), b.reshape(1, d))


def _first_index_of_max(vals, axis, size):
    mx = jnp.max(vals, axis=axis, keepdims=True)
    idx = lax.broadcasted_iota(jnp.int32, vals.shape, axis)
    first = jnp.min(jnp.where(vals == mx, idx, size), axis=axis, keepdims=True)
    return mx, idx == first


def _router_kernel(x_ref, wt_ref, b_ref, idx_ref, w_ref, rank_ref, cnt_ref, carry_ref):
    i = pl.program_id(0)

    @pl.when(i == 0)
    def _():
        carry_ref[...] = jnp.zeros_like(carry_ref)

    tm = x_ref.shape[0]
    logits = lax.dot_general(wt_ref[...], x_ref[...], (((1,), (1,)), ((), ())),
                             precision=lax.Precision.HIGHEST,
                             preferred_element_type=_F32)
    scores = jax.nn.sigmoid(logits)
    choice = scores + b_ref[...]
    c3 = choice.reshape(N_GROUPS, GROUP_SIZE, tm)
    m1, hit1 = _first_index_of_max(c3, 1, GROUP_SIZE)
    m2 = jnp.max(jnp.where(hit1, _NEG, c3), axis=1, keepdims=True)
    cur = (m1 + m2).reshape(N_GROUPS, tm)
    sel = jnp.zeros((N_GROUPS, tm), _F32)
    for _ in range(TOPK_GROUPS):
        _, hit = _first_index_of_max(cur, 0, N_GROUPS)
        sel = jnp.where(hit, 1.0, sel)
        cur = jnp.where(hit, _NEG, cur)
    exp_mask = jnp.broadcast_to(sel.reshape(N_GROUPS, 1, tm),
                                (N_GROUPS, GROUP_SIZE, tm)).reshape(N_EXPERTS, tm)
    masked = jnp.where(exp_mask > 0.0, choice, _NEG)

    eidx = lax.broadcasted_iota(jnp.int32, (N_EXPERTS, tm), 0)
    hits, idxs, ws = [], [], []
    member = jnp.zeros((N_EXPERTS, tm), _F32)
    for _ in range(TOP_K):
        _, hit = _first_index_of_max(masked, 0, N_EXPERTS)
        hits.append(hit)
        idxs.append(jnp.sum(jnp.where(hit, eidx, 0), axis=0, keepdims=True))
        ws.append(jnp.sum(jnp.where(hit, scores, 0.0), axis=0, keepdims=True))
        member = jnp.where(hit, 1.0, member)
        masked = jnp.where(hit, _NEG, masked)
    w = jnp.concatenate(ws, axis=0)
    w = w / jnp.sum(w, axis=0, keepdims=True) * ROUTED_SCALE
    idx_ref[...] = jnp.concatenate(idxs, axis=0)
    w_ref[...] = w

    r_i = lax.broadcasted_iota(jnp.int32, (tm, tm), 0)
    c_i = lax.broadcasted_iota(jnp.int32, (tm, tm), 1)
    before = jnp.where(r_i < c_i, 1.0, 0.0).astype(_BF16)
    prefix = jnp.dot(member.astype(_BF16), before, preferred_element_type=_F32)
    base = carry_ref[...] + prefix
    ranks = [jnp.sum(jnp.where(h, base, 0.0), axis=0, keepdims=True) for h in hits]
    rank_ref[...] = jnp.concatenate(ranks, axis=0).astype(jnp.int32)
    total = carry_ref[...] + jnp.sum(member, axis=1, keepdims=True)
    carry_ref[...] = total
    cnt_ref[...] = total.astype(jnp.int32)


def _router(x, w_r, b_r, tm=512):
    t, d = x.shape
    kt = pl.BlockSpec((TOP_K, tm), lambda i: (0, i))
    return pl.pallas_call(
        _router_kernel,
        grid=(t // tm,),
        in_specs=[pl.BlockSpec((tm, d), lambda i: (i, 0)),
                  pl.BlockSpec((N_EXPERTS, d), lambda i: (0, 0)),
                  pl.BlockSpec((N_EXPERTS, 1), lambda i: (0, 0))],
        out_specs=[kt, kt, kt, pl.BlockSpec((N_EXPERTS, 1), lambda i: (0, 0))],
        out_shape=[jax.ShapeDtypeStruct((TOP_K, t), jnp.int32),
                   jax.ShapeDtypeStruct((TOP_K, t), _F32),
                   jax.ShapeDtypeStruct((TOP_K, t), jnp.int32),
                   jax.ShapeDtypeStruct((N_EXPERTS, 1), jnp.int32)],
        scratch_shapes=[pltpu.VMEM((N_EXPERTS, 1), _F32)],
        compiler_params=_params(1),
        name="router",
    )(x, w_r.T, b_r.astype(_F32).reshape(N_EXPERTS, 1))


def _experts_kernel(te_ref, nv_ref, nxt_ref, nn_ref, clo_ref, chi_ref, par_ref, tok_ref,
                    xs_hbm, wg_hbm, wu_hbm, wd_hbm, y_ref,
                    xbuf, wg_b, wu_b, wd_b, stg_a, stg_d, gsem, sem_a, sem_d, *, tm, layer):
    i = pl.program_id(0)
    n_valid = nv_ref[0]
    d = wg_hbm.shape[2]
    ka = d // _W_CHUNKS
    kd = D_EXPERT // _W_CHUNKS
    n_chunks = 3 * _W_CHUNKS

    def chunk_copy(c, e, w_hbm, rows, stg, sem):
        s2 = c % 2
        r0 = pl.multiple_of((c % _W_CHUNKS) * rows, rows)
        return pltpu.make_async_copy(w_hbm.at[layer, e, pl.ds(r0, rows), :], stg.at[s2], sem.at[s2])

    def start_chunk(c, e):
        @pl.when(c < _W_CHUNKS)
        def _():
            chunk_copy(c, e, wg_hbm, ka, stg_a, sem_a).start(priority=1)

        @pl.when((c >= _W_CHUNKS) & (c < 2 * _W_CHUNKS))
        def _():
            chunk_copy(c, e, wu_hbm, ka, stg_a, sem_a).start(priority=1)

        @pl.when(c >= 2 * _W_CHUNKS)
        def _():
            chunk_copy(c, e, wd_hbm, kd, stg_d, sem_d).start(priority=1)

    def finish_chunk(c, e, slot):
        s2 = c % 2
        ra = pl.multiple_of((c % _W_CHUNKS) * ka, ka)
        rd = pl.multiple_of((c % _W_CHUNKS) * kd, kd)

        @pl.when(c < _W_CHUNKS)
        def _():
            chunk_copy(c, e, wg_hbm, ka, stg_a, sem_a).wait()
            wg_b[slot, pl.ds(ra, ka), :] = stg_a[s2].astype(_BF16)

        @pl.when((c >= _W_CHUNKS) & (c < 2 * _W_CHUNKS))
        def _():
            chunk_copy(c, e, wu_hbm, ka, stg_a, sem_a).wait()
            wu_b[slot, pl.ds(ra, ka), :] = stg_a[s2].astype(_BF16)

        @pl.when(c >= 2 * _W_CHUNKS)
        def _():
            chunk_copy(c, e, wd_hbm, kd, stg_d, sem_d).wait()
            wd_b[slot, pl.ds(rd, kd), :] = stg_d[s2].astype(_BF16)

    def load_chunks(lo, hi, e, slot):
        def body(c, carry):
            finish_chunk(c, e, slot)

            @pl.when(c + 2 < n_chunks)
            def _():
                start_chunk(c + 2, e)
            return carry
        lax.fori_loop(lo, hi, body, 0)

    def start_gather(tile, slot):
        base = tile * tm

        def body(r8, c):
            for u in range(_ISSUE_UNROLL):
                r = r8 * _ISSUE_UNROLL + u
                tok = tok_ref[base + r]
                pltpu.make_async_copy(xs_hbm.at[tok], xbuf.at[slot, r], gsem.at[slot]).start()
            return c
        lax.fori_loop(0, tm // _ISSUE_UNROLL, body, 0)

    @pl.when(i == 0)
    def _():
        start_gather(0, 0)
        start_chunk(0, te_ref[0])
        start_chunk(1, te_ref[0])
        load_chunks(0, n_chunks, te_ref[0], par_ref[0])

        @pl.when(nxt_ref[0] >= 0)
        def _():
            start_chunk(0, nxt_ref[0])
            start_chunk(1, nxt_ref[0])

    @pl.when(i + 1 < n_valid)
    def _():
        start_gather(i + 1, (i + 1) % 2)

    @pl.when(i < n_valid)
    def _():
        slot = par_ref[i]
        nxt = nxt_ref[i]
        lo = clo_ref[i]
        hi = chi_ref[i]
        third = (hi - lo + 2) // 3
        m1 = jnp.minimum(lo + third, hi)
        m2 = jnp.minimum(m1 + third, hi)

        def prefetch(c0, c1):
            @pl.when(nxt >= 0)
            def _():
                load_chunks(c0, c1, nxt, 1 - slot)

        prefetch(lo, m1)
        gslot = i % 2
        pltpu.make_async_copy(xs_hbm.at[pl.ds(0, tm)], xbuf.at[gslot], gsem.at[gslot]).wait()
        a = _from_slab(xbuf[gslot])
        g = jnp.dot(a, wg_b[slot], preferred_element_type=_F32)
        u = jnp.dot(a, wu_b[slot], preferred_element_type=_F32)
        prefetch(m1, m2)
        h = _silu_mul(g, u).astype(_BF16)
        y = jnp.dot(h, wd_b[slot], preferred_element_type=_F32)
        y_ref[...] = _to_slab(y.astype(_BF16))
        prefetch(m2, hi)

        @pl.when((nxt >= 0) & (hi == n_chunks) & (nn_ref[i] >= 0))
        def _():
            start_chunk(0, nn_ref[i])
            start_chunk(1, nn_ref[i])

    @pl.when(i >= n_valid)
    def _():
        y_ref[...] = jnp.zeros_like(y_ref)


def _experts(xs, meta, wg, wu, wd, layer, tm):
    t, sl, ln = xs.shape
    d = sl * ln
    n_tiles = meta[0].shape[0]
    ka = d // _W_CHUNKS
    kd = D_EXPERT // _W_CHUNKS
    any_spec = pl.BlockSpec(memory_space=pl.ANY)
    grid_spec = pltpu.PrefetchScalarGridSpec(
        num_scalar_prefetch=len(meta),
        grid=(n_tiles,),
        in_specs=[any_spec, any_spec, any_spec, any_spec],
        out_specs=pl.BlockSpec((tm, sl, ln), lambda i, *_: (i, 0, 0)),
        scratch_shapes=[pltpu.VMEM((2, tm, sl, ln), _BF16),
                        pltpu.VMEM((2, d, D_EXPERT), _BF16),
                        pltpu.VMEM((2, d, D_EXPERT), _BF16),
                        pltpu.VMEM((2, D_EXPERT, d), _BF16),
                        pltpu.VMEM((2, ka, D_EXPERT), _F32),
                        pltpu.VMEM((2, kd, d), _F32),
                        pltpu.SemaphoreType.DMA((2,)),
                        pltpu.SemaphoreType.DMA((2,)),
                        pltpu.SemaphoreType.DMA((2,))],
    )
    return pl.pallas_call(
        functools.partial(_experts_kernel, tm=tm, layer=layer),
        grid_spec=grid_spec,
        out_shape=jax.ShapeDtypeStruct((n_tiles * tm, sl, ln), _BF16),
        compiler_params=_params(1),
        name="experts",
    )(*meta, xs, wg, wu, wd)


def _ffn_kernel(a_ref, wg_ref, wu_ref, wd_ref, o_ref):
    a = a_ref[...]
    g = jnp.dot(a, wg_ref[...], preferred_element_type=_F32)
    u = jnp.dot(a, wu_ref[...], preferred_element_type=_F32)
    h = _silu_mul(g, u).astype(_BF16)
    o_ref[...] = jnp.dot(h, wd_ref[...], preferred_element_type=_F32)


def _shared_ffn(a, wg, wu, wd, tm=512):
    m, d = a.shape
    f = wg.shape[1]
    return pl.pallas_call(
        _ffn_kernel,
        grid=(m // tm,),
        in_specs=[pl.BlockSpec((tm, d), lambda i: (i, 0)),
                  pl.BlockSpec((d, f), lambda i: (0, 0)),
                  pl.BlockSpec((d, f), lambda i: (0, 0)),
                  pl.BlockSpec((f, d), lambda i: (0, 0))],
        out_specs=pl.BlockSpec((tm, d), lambda i: (i, 0)),
        out_shape=jax.ShapeDtypeStruct((m, d), _F32),
        compiler_params=_params(1),
        name="shared_ffn",
    )(a, wg, wu, wd)


def _combine_kernel(pos_ref, y_hbm, w_ref, sh_ref, x_ref, g_ref, b_ref, o_ref, ob_ref, os_ref,
                    ybuf, sem, *, tc, n_tok):
    i = pl.program_id(0)
    n_steps = pl.num_programs(0)

    def start_gather(step, slot):
        for k in range(TOP_K):
            base = k * n_tok + step * tc

            def body(r8, c, k=k, base=base):
                for u in range(_ISSUE_UNROLL):
                    r = r8 * _ISSUE_UNROLL + u
                    p = pos_ref[base + r]
                    pltpu.make_async_copy(y_hbm.at[p], ybuf.at[slot, k, r],
                                          sem.at[slot]).start(priority=u % 2)
                return c
            lax.fori_loop(0, tc // _ISSUE_UNROLL, body, 0)

    @pl.when(i == 0)
    def _():
        start_gather(0, 0)

    @pl.when(i + 1 < n_steps)
    def _():
        start_gather(i + 1, (i + 1) % 2)

    slot = i % 2
    for k in range(TOP_K):
        pltpu.make_async_copy(y_hbm.at[pl.ds(0, tc)], ybuf.at[slot, k], sem.at[slot]).wait()
    w = w_ref[...]
    acc = jnp.zeros(ybuf.shape[2:], _F32)
    for k in range(TOP_K):
        acc = acc + w[:, k:k + 1].reshape(tc, 1, 1) * ybuf[slot, k].astype(_F32)
    v = ALPHA * x_ref[...] + (_from_slab(acc) + sh_ref[...])
    y = _layer_norm_rows(v, g_ref[...], b_ref[...])
    yb = y.astype(_BF16)
    o_ref[...] = y
    ob_ref[...] = yb
    os_ref[...] = _to_slab(yb)


def _combine(pos_flat, ys, w_tok, shared, x, g, b, tc=64):
    t, d = x.shape
    sl = d // _LANES
    row = pl.BlockSpec((tc, d), lambda i, pos: (i, 0))
    slab = pl.BlockSpec((tc, sl, _LANES), lambda i, pos: (i, 0, 0))
    vec = pl.BlockSpec((1, d), lambda i, pos: (0, 0))
    grid_spec = pltpu.PrefetchScalarGridSpec(
        num_scalar_prefetch=1,
        grid=(t // tc,),
        in_specs=[pl.BlockSpec(memory_space=pl.ANY),
                  pl.BlockSpec((tc, TOP_K), lambda i, pos: (i, 0)),
                  row, row, vec, vec],
        out_specs=[row, row, slab],
        scratch_shapes=[pltpu.VMEM((2, TOP_K, tc, sl, _LANES), _BF16),
                        pltpu.SemaphoreType.DMA((2,))],
    )
    return pl.pallas_call(
        functools.partial(_combine_kernel, tc=tc, n_tok=t),
        grid_spec=grid_spec,
        out_shape=[jax.ShapeDtypeStruct((t, d), _F32), jax.ShapeDtypeStruct((t, d), _BF16),
                   jax.ShapeDtypeStruct((t, sl, _LANES), _BF16)],
        compiler_params=_params(1),
        name="combine",
    )(pos_flat, ys, w_tok, shared, x, g.reshape(1, d), b.reshape(1, d))


def _row_table_kernel(pos_ref, lo_ref, hi_ref, tab_ref, *, n_tok):
    def clear_expert(e, c):
        def clear(r, c2):
            tab_ref[r] = 0
            return c2
        lax.fori_loop(lo_ref[e], hi_ref[e], clear, 0)
        return c
    lax.fori_loop(0, lo_ref.shape[0], clear_expert, 0)

    for k in range(TOP_K):
        def fill(i, c, k=k):
            for u in range(_ISSUE_UNROLL):
                tkn = i * _ISSUE_UNROLL + u
                tab_ref[pos_ref[k * n_tok + tkn]] = tkn
            return c
        lax.fori_loop(0, n_tok // _ISSUE_UNROLL, fill, 0)


def _row_table(pos_flat, pad_lo, pad_hi, n_rows, n_tok):
    smem = pl.BlockSpec(memory_space=pltpu.SMEM)
    return pl.pallas_call(
        functools.partial(_row_table_kernel, n_tok=n_tok),
        in_specs=[smem, smem, smem],
        out_specs=smem,
        out_shape=jax.ShapeDtypeStruct((n_rows,), jnp.int32),
        name="row_table",
    )(pos_flat, pad_lo, pad_hi)


def _moe_block(x, xb, xs, layer, w_r, b_r, wg, wu, wd, sg, su, sd, g, b, tm=256):
    t, d = x.shape
    idx_t, w_t, rank_t, cnt = _router(x, w_r, b_r)

    counts = cnt[:, 0]
    tiles_e = (counts + tm - 1) // tm
    tile_end = jnp.cumsum(tiles_e)
    tile_start = tile_end - tiles_e
    row_off = tile_start * tm
    n_valid = tile_end[-1:]
    n_tiles = t * TOP_K // tm + N_EXPERTS
    e_ids = jnp.arange(N_EXPERTS, dtype=jnp.int32)
    pos = jnp.sum(jnp.where(idx_t[:, :, None] == e_ids, row_off, 0), axis=-1) + rank_t
    pad_hi = jnp.concatenate([row_off[1:], jnp.full((1,), n_tiles * tm, jnp.int32)])
    row_token = _row_table(pos.reshape(-1), row_off + counts, pad_hi, n_tiles * tm, t)
    tile_ids = jnp.minimum(jnp.arange(n_tiles, dtype=jnp.int32), n_valid - 1)
    onehot_te = (tile_end[None, :] <= tile_ids[:, None]).astype(jnp.int32)
    tile_expert = jnp.sum(onehot_te, axis=1)

    has = tiles_e > 0
    later = (e_ids[None, :] > e_ids[:, None]) & has[None, :]
    nxt_e = jnp.min(jnp.where(later, e_ids[None, :], N_EXPERTS), axis=1)
    nxt_e = jnp.where(nxt_e == N_EXPERTS, -1, nxt_e)
    nxt_onehot = nxt_e[:, None] == e_ids[None, :]
    nn_e = jnp.where(nxt_e >= 0, jnp.sum(jnp.where(nxt_onehot, nxt_e[None, :], 0), axis=1), -1)
    par_e = (jnp.cumsum(has.astype(jnp.int32)) - has.astype(jnp.int32)) % 2

    def per_tile(v):
        return jnp.sum(jnp.where(tile_expert[:, None] == e_ids[None, :], v[None, :], 0), axis=1)
    n_run = jnp.maximum(per_tile(tiles_e), 1)
    p_run = tile_ids - per_tile(tile_start)
    n_chunks = 3 * _W_CHUNKS
    meta = (tile_expert, n_valid.astype(jnp.int32), per_tile(nxt_e), per_tile(nn_e),
            (n_chunks * p_run) // n_run, (n_chunks * (p_run + 1)) // n_run, per_tile(par_e),
            row_token)
    meta = tuple(m.astype(jnp.int32) for m in meta)

    ys = _experts(xs, meta, wg, wu, wd, layer, tm)
    shared = _shared_ffn(xb, sg.astype(_BF16), su.astype(_BF16), sd.astype(_BF16))
    return _combine(pos.reshape(-1), ys, w_t.T, shared, x, g, b)


def _fgate_kernel(x_ref, wt_ref, b_ref, c_ref, carry_ref):
    s = pl.program_id(1)

    @pl.when(s == 0)
    def _():
        carry_ref[...] = jnp.zeros_like(carry_ref)

    ts = x_ref.shape[0]
    z = lax.dot_general(wt_ref[...], x_ref[...], (((1,), (1,)), ((), ())),
                        precision=lax.Precision.HIGHEST,
                        preferred_element_type=_F32) + b_ref[...]
    ls = jnp.minimum(z, 0.0) - jnp.log1p(jnp.exp(-jnp.abs(z)))
    r_i = lax.broadcasted_iota(jnp.int32, (ts, ts), 0)
    c_i = lax.broadcasted_iota(jnp.int32, (ts, ts), 1)
    upto = jnp.where(r_i <= c_i, 1.0, 0.0)
    c = carry_ref[...] + jnp.dot(ls, upto, precision=lax.Precision.HIGHEST,
                                 preferred_element_type=_F32)
    c_ref[...] = c
    carry_ref[...] = c[:, ts - 1:ts]


def _fgate(x3, w_f, b_f, ts=512):
    bsz, s, d = x3.shape
    return pl.pallas_call(
        _fgate_kernel,
        grid=(bsz, s // ts),
        in_specs=[pl.BlockSpec((None, ts, d), lambda bb, i: (bb, i, 0)),
                  pl.BlockSpec((N_HEADS, d), lambda bb, i: (0, 0)),
                  pl.BlockSpec((N_HEADS, 1), lambda bb, i: (0, 0))],
        out_specs=pl.BlockSpec((None, N_HEADS, ts), lambda bb, i: (bb, 0, i)),
        out_shape=jax.ShapeDtypeStruct((bsz, N_HEADS, s), _F32),
        scratch_shapes=[pltpu.VMEM((N_HEADS, 1), _F32)],
        compiler_params=_params(2),
        name="fgate",
    )(x3, w_f.T, b_f.astype(_F32).reshape(N_HEADS, 1))


def _attn_kernel(q_ref, k_ref, v_ref, cq_ref, ck_ref, o_ref, *, tq, G):
    hp = pl.program_id(1)
    qi = pl.program_id(2)
    qk_scale = HEAD_DIM ** -0.5 * _LOG2E
    lane = lax.broadcasted_iota(jnp.int32, cq_ref.shape, 1)
    cqs = [_LOG2E * jnp.sum(jnp.where(lane == hp * G + g, cq_ref[...], 0.0), axis=1, keepdims=True)
           for g in range(G)]
    qs = [q_ref[:, g * HEAD_DIM:(g + 1) * HEAD_DIM] for g in range(G)]

    def step(j, carry, diagonal):
        k0 = pl.multiple_of(j * tq, tq)
        out = []
        for g in range(G):
            m, l, acc = carry[g]
            kj = k_ref[pl.ds(k0, tq), g * HEAD_DIM:(g + 1) * HEAD_DIM]
            vj = v_ref[pl.ds(k0, tq), g * HEAD_DIM:(g + 1) * HEAD_DIM]
            ck = _LOG2E * ck_ref[g:g + 1, pl.ds(k0, tq)]
            s = lax.dot_general(qs[g], kj, (((1,), (1,)), ((), ())), preferred_element_type=_F32)
            s = s * qk_scale - ck
            if diagonal:
                r_i = lax.broadcasted_iota(jnp.int32, s.shape, 0)
                c_i = lax.broadcasted_iota(jnp.int32, s.shape, 1)
                s = jnp.where(c_i <= r_i, s, _NEG)
            m_new = jnp.maximum(m, jnp.max(s, axis=1, keepdims=True) + cqs[g])
            alpha = jnp.exp2(m - m_new)
            p = jnp.exp2(s - (m_new - cqs[g]))
            l = alpha * l + jnp.sum(p, axis=1, keepdims=True)
            acc = alpha * acc + jnp.dot(p.astype(_BF16), vj, preferred_element_type=_F32)
            out.append((m_new, l, acc))
        return tuple(out)

    init = tuple((jnp.full((tq, 1), _NEG, _F32), jnp.zeros((tq, 1), _F32),
                  jnp.zeros((tq, HEAD_DIM), _F32)) for _ in range(G))
    carry = lax.fori_loop(0, qi, lambda j, c: step(j, c, False), init)
    fin = step(qi, carry, True)
    for g in range(G):
        _, l, acc = fin[g]
        o_ref[:, g * HEAD_DIM:(g + 1) * HEAD_DIM] = (acc / l).astype(o_ref.dtype)


def _attention(q3, kv3, c_sh, c_hs, tq=1024, G=_HEADS_PER_STEP):
    bsz, s, d = q3.shape
    w = G * HEAD_DIM
    nh = N_HEADS // G
    return pl.pallas_call(
        functools.partial(_attn_kernel, tq=tq, G=G),
        grid=(bsz, nh, s // tq),
        in_specs=[pl.BlockSpec((None, tq, w), lambda bb, h, i: (bb, i, h)),
                  pl.BlockSpec((None, s, w), lambda bb, h, i: (bb, 0, h)),
                  pl.BlockSpec((None, s, w), lambda bb, h, i: (bb, 0, nh + h)),
                  pl.BlockSpec((None, tq, N_HEADS), lambda bb, h, i: (bb, i, 0)),
                  pl.BlockSpec((None, None, G, s), lambda bb, h, i: (bb, h, 0, 0))],
        out_specs=pl.BlockSpec((None, tq, w), lambda bb, h, i: (bb, i, h)),
        out_shape=jax.ShapeDtypeStruct((bsz, s, d), _BF16),
        compiler_params=_params(3),
        name="fox_attention",
    )(q3, kv3, kv3, c_sh, c_hs.reshape(bsz, nh, G, s))


def kernel(x, ln_g, ln_b, conv_w_in, conv_w, conv_w_out, kv_w, kv_fb, attn_w_q, attn_w_o,
           router_w, router_b, exp_w_gate, exp_w_up, exp_w_down,
           shared_w_gate, shared_w_up, shared_w_down):
    bsz, seq, d = x.shape
    t = bsz * seq
    xf = x.reshape(t, d)
    xb = xf.astype(_BF16)
    kv3 = c_sh = c_hs = None
    for l in range(DEPTH):
        if l < N_A_LAYERS:
            y = _conv_in(xb, conv_w_in, conv_w, l, seq)
            h = _mm(y, conv_w_out, l, d, _F32)
        else:
            j = l - N_A_LAYERS
            if l == N_A_LAYERS:
                kv = _mm(xb, kv_w, None, 2 * d, _BF16)
                kv3 = kv.reshape(bsz, seq, 2 * d)
                c_hs = _fgate(xf.reshape(bsz, seq, d), kv_w[:, 2 * d:], kv_fb)
                c_sh = jnp.transpose(c_hs, (0, 2, 1))
            q = _mm(xb, attn_w_q, j, d, _BF16)
            o = _attention(q.reshape(bsz, seq, d), kv3, c_sh, c_hs)
            h = _mm(o.reshape(t, d), attn_w_o, j, d, _F32)
        xf, xb, xs = _res_ln(xf, h, ln_g[l, 0], ln_b[l, 0])
        xf, xb, xs = _moe_block(xf, xb, xs, l, router_w[l], router_b[l],
                                exp_w_gate, exp_w_up, exp_w_down,
                                shared_w_gate[l], shared_w_up[l], shared_w_down[l],
                                ln_g[l, 1], ln_b[l, 1])
    return xf.reshape(bsz, seq, d)
```
